```python
import math
import jax, jax.numpy as jnp
from jax import lax
import numpy as np

D_MODEL = 1024
BATCH = 4
SEQ = 4096
DEPTH = 2

HEAD_DIM = 64
SB_HEADS = D_MODEL // (2 * HEAD_DIM)
DA_HEADS = D_MODEL // (4 * HEAD_DIM)
SB_WIDTH = SB_HEADS * HEAD_DIM
DA_WIDTH = DA_HEADS * 2 * HEAD_DIM
MIX_WIDTH = SB_WIDTH + DA_WIDTH
QKV_WIDTH = 3 * SB_WIDTH + 3 * DA_WIDTH
ROPE_DIM = HEAD_DIM // 4
ROPE_THETA = 500000.0
BLOCK_Q = 128
N_EXPERTS = 16
N_GROUPS = 4
EXPERTS_PER_GROUP = N_EXPERTS // N_GROUPS
TOP_K = 2
D_FF_EXPERT = 512
PLE_DIM = 256
DEEPNORM_ALPHA = (2 * DEPTH) ** 0.25
DEEPNORM_BETA = (8 * DEPTH) ** -0.25
LN_EPS = 1e-5

kernel_name = "hymba_stickbreak_diffattn_grouped_moe_deepnorm"


def layer_norm(x, g, b):
    xf = x.astype(jnp.float32)
    mu = jnp.mean(xf, axis=-1, keepdims=True)
    var = jnp.mean(jnp.square(xf - mu), axis=-1, keepdims=True)
    y = (xf - mu) * lax.rsqrt(var + LN_EPS) * g.astype(jnp.float32) + b.astype(jnp.float32)
    return y.astype(x.dtype)


def head_rms_norm(x, g, scale=1.0):
    xf = x.astype(jnp.float32)
    y = xf * lax.rsqrt(jnp.mean(jnp.square(xf), axis=-1, keepdims=True) + LN_EPS)
    return (y * (g.astype(jnp.float32) * scale)).astype(x.dtype)


def rope_tables(positions):
    inv_freq = ROPE_THETA ** (-jnp.arange(0, ROPE_DIM, 2, dtype=jnp.float32) / ROPE_DIM)
    ang = positions.astype(jnp.float32)[..., None] * inv_freq
    return jnp.cos(ang), jnp.sin(ang)


def apply_partial_rope(x, cos, sin):
    half = ROPE_DIM // 2
    x1 = x[..., :half].astype(jnp.float32)
    x2 = x[..., half:ROPE_DIM].astype(jnp.float32)
    rot = jnp.concatenate([x1 * cos - x2 * sin, x2 * cos + x1 * sin], axis=-1).astype(x.dtype)
    return jnp.concatenate([rot, x[..., ROPE_DIM:]], axis=-1)


def stick_breaking_attention(q, k, v):
    seq, d = q.shape[2], q.shape[3]
    scale = d ** -0.5
    outs = []
    for blk in range(seq // BLOCK_Q):
        t0, t1 = blk * BLOCK_Q, (blk + 1) * BLOCK_Q
        z = jnp.einsum('bhqd,bhkd->bhqk', q[:, :, t0:t1], k[:, :, :t1]).astype(jnp.float32) * scale
        qpos = t0 + jnp.arange(BLOCK_Q)
        kpos = jnp.arange(t1)
        mask = kpos[None, :] < qpos[:, None]
        log_not = jnp.where(mask, -jax.nn.softplus(z), 0.0)
        tail = lax.cumsum(log_not, axis=3, reverse=True) - log_not
        a = jnp.where(mask, jnp.exp(jax.nn.log_sigmoid(z) + tail), 0.0)
        outs.append(jnp.einsum('bhqk,bhkd->bhqd', a.astype(v.dtype), v[:, :, :t1]))
    return jnp.concatenate(outs, axis=2)


def differential_attention(q, k, v, lam):
    seq, d = q.shape[3], q.shape[4]
    scale = d ** -0.5
    outs = []
    for blk in range(seq // BLOCK_Q):
        t0, t1 = blk * BLOCK_Q, (blk + 1) * BLOCK_Q
        s = jnp.einsum('bhcqd,bhckd->bhcqk', q[:, :, :, t0:t1], k[:, :, :, :t1]).astype(jnp.float32) * scale
        mask = jnp.arange(t1)[None, :] <= (t0 + jnp.arange(BLOCK_Q))[:, None]
        prob = jax.nn.softmax(jnp.where(mask, s, -jnp.inf), axis=-1)
        w = prob[:, :, 0] - lam * prob[:, :, 1]
        outs.append(jnp.einsum('bhqk,bhkd->bhqd', w.astype(v.dtype), v[:, :, :t1]))
    return jnp.concatenate(outs, axis=2)


def grouped_moe(x, router_w, router_b, w_gate, w_up, w_down):
    b, s, d = x.shape
    xt = x.reshape(b * s, d)
    scores = jax.nn.sigmoid((xt @ router_w).astype(jnp.float32))
    sel = (scores + router_b.astype(jnp.float32)).reshape(-1, N_GROUPS, EXPERTS_PER_GROUP)
    group_score = lax.top_k(sel, TOP_K)[0].sum(-1)
    grp = jnp.argmax(group_score, axis=-1)
    in_grp = jnp.take_along_axis(sel, grp[:, None, None], axis=1)[:, 0]
    _, local = lax.top_k(in_grp, TOP_K)
    idx = grp[:, None] * EXPERTS_PER_GROUP + local
    gate = jnp.take_along_axis(scores, idx, axis=-1)
    gate = gate / jnp.sum(gate, axis=-1, keepdims=True)
    combine = jnp.sum(jax.nn.one_hot(idx, N_EXPERTS, dtype=jnp.float32) * gate[..., None], axis=1)
    y = jnp.zeros((b * s, d), jnp.float32)
    for e in range(N_EXPERTS):
        h = jax.nn.silu(xt @ w_gate[e]) * (xt @ w_up[e])
        y = y + combine[:, e:e + 1] * (h @ w_down[e]).astype(jnp.float32)
    return y.reshape(b, s, d).astype(x.dtype)


def setup_inputs(seed: int = 0) -> dict:
    key = jax.random.key(seed)
    ks = jax.random.split(key, 20)
    f32 = jnp.float32
    nrm = lambda k, shape, std: jax.random.normal(k, shape, f32) * std
    offset = jax.random.randint(ks[2], (BATCH, 1), 0, 1024, dtype=jnp.int32)
    positions = offset + jnp.arange(SEQ, dtype=jnp.int32)[None, :]
    return {
        "x": nrm(ks[0], (BATCH, SEQ, D_MODEL), 1.0),
        "p": nrm(ks[1], (DEPTH, BATCH, SEQ, PLE_DIM), 1.0),
        "positions": positions,
        "w_in": nrm(ks[3], (DEPTH, D_MODEL, QKV_WIDTH), D_MODEL ** -0.5),
        "w_o": nrm(ks[4], (DEPTH, MIX_WIDTH, D_MODEL), MIX_WIDTH ** -0.5 * DEEPNORM_BETA),
        "sb_norm_g": 1.0 + nrm(ks[5], (DEPTH, HEAD_DIM), 0.02),
        "da_lambda": nrm(ks[6], (DEPTH, 4, HEAD_DIM), 0.1),
        "da_subln_g": 1.0 + nrm(ks[7], (DEPTH, 2 * HEAD_DIM), 0.02),
        "ln1_g": 1.0 + nrm(ks[8], (DEPTH, D_MODEL), 0.02),
        "ln1_b": nrm(ks[9], (DEPTH, D_MODEL), 0.02),
        "ln2_g": 1.0 + nrm(ks[10], (DEPTH, D_MODEL), 0.02),
        "ln2_b": nrm(ks[11], (DEPTH, D_MODEL), 0.02),
        "router_w": nrm(ks[12], (D_MODEL, N_EXPERTS), D_MODEL ** -0.5),
        "router_b": nrm(ks[13], (N_EXPERTS,), 0.01),
        "w_gate": nrm(ks[14], (DEPTH, N_EXPERTS, D_MODEL, D_FF_EXPERT), D_MODEL ** -0.5),
        "w_up": nrm(ks[15], (DEPTH, N_EXPERTS, D_MODEL, D_FF_EXPERT), D_MODEL ** -0.5 * DEEPNORM_BETA),
        "w_down": nrm(ks[16], (DEPTH, N_EXPERTS, D_FF_EXPERT, D_MODEL), D_FF_EXPERT ** -0.5 * DEEPNORM_BETA),
        "w_ple": nrm(ks[17], (DEPTH, PLE_DIM, D_MODEL), PLE_DIM ** -0.5),
        "w_ple_gate": nrm(ks[18], (DEPTH, D_MODEL, D_MODEL), D_MODEL ** -0.5),
        "b_ple_gate": nrm(ks[19], (DEPTH, D_MODEL), 0.02),
    }


def reference(x, p, positions, w_in, w_o, sb_norm_g, da_lambda, da_subln_g, ln1_g, ln1_b,
              ln2_g, ln2_b, router_w, router_b, w_gate, w_up, w_down, w_ple, w_ple_gate, b_ple_gate):
    b, s, _ = x.shape
    cos, sin = rope_tables(positions)
    cos_da, sin_da = cos[:, None, None], sin[:, None, None]
    for i in range(DEPTH):
        lambda_init = 0.8 - 0.6 * math.exp(-0.3 * i)
        h = x @ w_in[i]
        o = 0
        sb_q, sb_k, sb_v = (h[..., o + j * SB_WIDTH:o + (j + 1) * SB_WIDTH] for j in range(3))
        o = 3 * SB_WIDTH
        da_q, da_k, da_v = (h[..., o + j * DA_WIDTH:o + (j + 1) * DA_WIDTH] for j in range(3))

        to_heads = lambda t: t.reshape(b, s, SB_HEADS, HEAD_DIM).transpose(0, 2, 1, 3)
        sb_out = stick_breaking_attention(to_heads(sb_q), to_heads(sb_k), to_heads(sb_v))
        sb_out = head_rms_norm(sb_out.transpose(0, 2, 1, 3), sb_norm_g[i]).reshape(b, s, SB_WIDTH)

        to_qk = lambda t: t.reshape(b, s, DA_HEADS, 2, HEAD_DIM).transpose(0, 2, 3, 1, 4)
        dq = apply_partial_rope(to_qk(da_q), cos_da, sin_da)
        dk = apply_partial_rope(to_qk(da_k), cos_da, sin_da)
        dv = da_v.reshape(b, s, DA_HEADS, 2 * HEAD_DIM).transpose(0, 2, 1, 3)
        lam_p = da_lambda[i].astype(jnp.float32)
        lam = (jnp.exp(jnp.sum(lam_p[0] * lam_p[1])) - jnp.exp(jnp.sum(lam_p[2] * lam_p[3]))
               + lambda_init)
        da_out = differential_attention(dq, dk, dv, lam)
        da_out = head_rms_norm(da_out.transpose(0, 2, 1, 3), da_subln_g[i], 1.0 - lambda_init)
        da_out = da_out.reshape(b, s, DA_WIDTH)

        mix = jnp.concatenate([sb_out, da_out], axis=-1) @ w_o[i]
        x = layer_norm(DEEPNORM_ALPHA * x + mix, ln1_g[i], ln1_b[i])

        ffn = grouped_moe(x, router_w, router_b, w_gate[i], w_up[i], w_down[i])
        x = layer_norm(DEEPNORM_ALPHA * x + ffn, ln2_g[i], ln2_b[i])

        gate = jax.nn.sigmoid(x @ w_ple_gate[i] + b_ple_gate[i])
        x = x + gate * (p[i] @ w_ple[i])
    return x
```

```python
import functools
import math

import numpy as np
import jax
import jax.numpy as jnp
from jax import lax
from jax.experimental import pallas as pl
from jax.experimental.pallas import tpu as pltpu

D_MODEL = 1024
HEAD_DIM = 64
SB_WIDTH = 512
DA_WIDTH = 512
DA_HEADS = 4
ROPE_DIM = 16
ROPE_THETA = 500000.0
N_EXPERTS = 16
N_GROUPS = 4
EXPERTS_PER_GROUP = 4
D_FF = 512
PLE_DIM = 256
LN_EPS = 1e-5

LANES = 128
N_PAIRS = 6
N_CLASSES = N_GROUPS * N_PAIRS
CLASS_ROWS = 32
GATE_COLS = LANES
ROW_W = D_MODEL + GATE_COLS

MOE_TILE = 256
BQ = 256
BK = 256
TM_QKV = 512
TM_ROUTE = 512
TM_ROWS = 512
VMEM_LIMIT = 56 * 1024 * 1024

F32 = jnp.float32
BF16 = jnp.bfloat16

_PAIR_LOCAL = [(0, 1), (0, 2), (1, 2), (0, 3), (1, 3), (2, 3)]
_CLASS_E1 = np.array([g * 4 + _PAIR_LOCAL[p][0] for g in range(N_GROUPS) for p in range(N_PAIRS)], np.int32)
_CLASS_E2 = np.array([g * 4 + _PAIR_LOCAL[p][1] for g in range(N_GROUPS) for p in range(N_PAIRS)], np.int32)


def _dot(a, b):
    return jnp.dot(a, b, preferred_element_type=F32)


def _dot_nt(a, b, precision=None):
    return lax.dot_general(a, b, (((1,), (1,)), ((), ())), precision=precision,
                           preferred_element_type=F32)


def _sigmoid(x):
    return 1.0 / (1.0 + jnp.exp(-x))


def _layer_norm(y, g, b):
    mu = jnp.mean(y, axis=-1, keepdims=True)
    yc = y - mu
    var = jnp.mean(yc * yc, axis=-1, keepdims=True)
    return yc * lax.rsqrt(var + LN_EPS) * g + b


def _rope_table_kernel(pos_ref, invf_ref, cos_ref, sin_ref):
    ang = pos_ref[...].astype(F32) * invf_ref[...]
    cos_ref[...] = jnp.cos(ang)
    sin_ref[...] = jnp.sin(ang)


def _rope_tables(pos_col, invf_row):
    n = pos_col.shape[0]
    tm = 1024
    return pl.pallas_call(
        _rope_table_kernel,
        out_shape=(jax.ShapeDtypeStruct((n, LANES), F32), jax.ShapeDtypeStruct((n, LANES), F32)),
        grid=(n // tm,),
        in_specs=[pl.BlockSpec((tm, 1), lambda i: (i, 0)),
                  pl.BlockSpec((1, LANES), lambda i: (0, 0))],
        out_specs=(pl.BlockSpec((tm, LANES), lambda i: (i, 0)),
                   pl.BlockSpec((tm, LANES), lambda i: (i, 0))),
        compiler_params=pltpu.CompilerParams(dimension_semantics=("arbitrary",)),
        name="rope_tables",
    )(pos_col, invf_row)


def _qkv_kernel(x_ref, w_ref, cos_ref, sin_ref, sbq, sbk, sbv, daq, dak, dav):
    xb = x_ref[...].astype(BF16)
    lane = lax.broadcasted_iota(jnp.int32, (1, LANES), 1) % HEAD_DIM
    half = ROPE_DIM // 2
    cosr = jnp.where(lane < ROPE_DIM, cos_ref[...], 1.0)
    sin_lo = jnp.where(lane < half, -sin_ref[...], 0.0)
    sin_hi = jnp.where((lane >= half) & (lane < ROPE_DIM), sin_ref[...], 0.0)
    scale = HEAD_DIM ** -0.5
    outs = (sbq, sbk, sbv, daq, dak, dav)
    for c, o_ref in enumerate(outs):
        hc = _dot(xb, w_ref[:, c * SB_WIDTH:(c + 1) * SB_WIDTH])
        for blk in range(SB_WIDTH // LANES):
            h = hc[:, blk * LANES:(blk + 1) * LANES]
            if c in (3, 4):
                h = (h * cosr + pltpu.roll(h, LANES - half, 1) * sin_lo
                     + pltpu.roll(h, half, 1) * sin_hi)
            if c in (0, 3):
                h = h * scale
            o_ref[:, blk * LANES:(blk + 1) * LANES] = h.astype(BF16)


def _qkv(x2d, w_bf16, cos_t, sin_t):
    n = x2d.shape[0]
    tm = TM_QKV
    row = lambda i: (i, 0)
    out_sd = jax.ShapeDtypeStruct((n, SB_WIDTH), BF16)
    return pl.pallas_call(
        _qkv_kernel,
        out_shape=(out_sd,) * 6,
        grid=(n // tm,),
        in_specs=[pl.BlockSpec((tm, D_MODEL), row),
                  pl.BlockSpec(w_bf16.shape, lambda i: (0, 0)),
                  pl.BlockSpec((tm, LANES), row),
                  pl.BlockSpec((tm, LANES), row)],
        out_specs=(pl.BlockSpec((tm, SB_WIDTH), row),) * 6,
        compiler_params=pltpu.CompilerParams(dimension_semantics=("arbitrary",),
                                             vmem_limit_bytes=VMEM_LIMIT),
        name="qkv",
    )(x2d, w_bf16, cos_t, sin_t)


def _sb_kernel(q_ref, k_ref, v_ref, u_ref, g_ref, o_ref):
    qi = pl.program_id(2)
    q = q_ref[...]
    lane = lax.broadcasted_iota(jnp.int32, (1, LANES), 1)
    row = lax.broadcasted_iota(jnp.int32, (BQ, BK), 0)
    col = lax.broadcasted_iota(jnp.int32, (BQ, BK), 1)
    strict = col < row
    u = u_ref[...]

    def block(qh, kb, carry, acc, mask):
        start = pl.multiple_of(kb * BK, BK)
        ks = k_ref[pl.ds(start, BK), :]
        vs = v_ref[pl.ds(start, BK), :]
        z = _dot_nt(qh, ks)
        lg = -(jnp.maximum(z, 0.0) + jnp.log1p(jnp.exp(-jnp.abs(z))))
        if mask is not None:
            lg = jnp.where(mask, lg, 0.0)
        lg_hi = lg.astype(BF16)
        lg_lo = (lg - lg_hi.astype(F32)).astype(BF16)
        tail = _dot(lg_hi, u) + _dot(lg_lo, u)
        a = jnp.exp(z + tail + carry)
        if mask is not None:
            a = jnp.where(mask, a, 0.0)
        acc = acc + _dot(a.astype(BF16), vs)
        carry = carry + jnp.sum(lg, axis=1, keepdims=True)
        return carry, acc

    accs = []
    for h in range(2):
        hmask = (lane >= h * HEAD_DIM) & (lane < (h + 1) * HEAD_DIM)
        qh = jnp.where(hmask, q, jnp.zeros_like(q))
        carry0 = jnp.zeros((BQ, 1), F32)
        acc0 = jnp.zeros((BQ, LANES), F32)
        carry, acc = block(qh, qi, carry0, acc0, strict)

        def body(i, st, qh=qh):
            return block(qh, qi - 1 - i, st[0], st[1], None)

        carry, acc = lax.fori_loop(0, qi, body, (carry, acc))
        accs.append(acc)

    first = lane < HEAD_DIM
    o = jnp.where(first, accs[0], accs[1])
    sq = o * o
    ss0 = jnp.sum(jnp.where(first, sq, 0.0), axis=1, keepdims=True)
    ss1 = jnp.sum(jnp.where(first, 0.0, sq), axis=1, keepdims=True)
    ms = jnp.where(first, ss0, ss1) * (1.0 / HEAD_DIM)
    o_ref[...] = (o * lax.rsqrt(ms + LN_EPS) * g_ref[...]).astype(BF16)


def _sb_attention(q, k, v, u_tri, g_row, batch, seq):
    nq = seq // BQ
    n_pairs = SB_WIDTH // LANES
    return pl.pallas_call(
        _sb_kernel,
        out_shape=jax.ShapeDtypeStruct(q.shape, BF16),
        grid=(batch, n_pairs, nq),
        in_specs=[pl.BlockSpec((BQ, LANES), lambda b, h, i: (b * nq + i, h)),
                  pl.BlockSpec((seq, LANES), lambda b, h, i: (b, h)),
                  pl.BlockSpec((seq, LANES), lambda b, h, i: (b, h)),
                  pl.BlockSpec((BK, BK), lambda b, h, i: (0, 0)),
                  pl.BlockSpec((1, LANES), lambda b, h, i: (0, 0))],
        out_specs=pl.BlockSpec((BQ, LANES), lambda b, h, i: (b * nq + i, h)),
        compiler_params=pltpu.CompilerParams(
            dimension_semantics=("arbitrary", "arbitrary", "arbitrary"), vmem_limit_bytes=VMEM_LIMIT),
        name="sb_attn",
    )(q, k, v, u_tri, g_row)


def _da_kernel(q_ref, k_ref, v_ref, lam_ref, g_ref, o_ref, *, lambda_init):
    qi = pl.program_id(2)
    q = q_ref[...]
    lane = lax.broadcasted_iota(jnp.int32, (1, LANES), 1)
    row = lax.broadcasted_iota(jnp.int32, (BQ, BK), 0)
    col = lax.broadcasted_iota(jnp.int32, (BQ, BK), 1)
    causal = col <= row

    def block(qc, kb, m, l, acc, mask):
        start = pl.multiple_of(kb * BK, BK)
        ks = k_ref[pl.ds(start, BK), :]
        vs = v_ref[pl.ds(start, BK), :]
        s = _dot_nt(qc, ks)
        if mask is not None:
            s = jnp.where(mask, s, -jnp.inf)
        m_blk = jnp.max(s, axis=1, keepdims=True)
        m_new = m_blk if m is None else jnp.maximum(m, m_blk)
        p = jnp.exp(s - m_new)
        pv = _dot(p.astype(BF16), vs)
        ps = jnp.sum(p, axis=1, keepdims=True)
        if m is None:
            return m_new, ps, pv
        alpha = jnp.exp(m - m_new)
        return m_new, alpha * l + ps, alpha * acc + pv

    outs = []
    for c in range(2):
        cmask = (lane >= c * HEAD_DIM) & (lane < (c + 1) * HEAD_DIM)
        qc = jnp.where(cmask, q, jnp.zeros_like(q))
        m, l, acc = block(qc, qi, None, None, None, causal)

        def body(kb, st, qc=qc):
            return block(qc, kb, st[0], st[1], st[2], None)

        m, l, acc = lax.fori_loop(0, qi, body, (m, l, acc))
        outs.append(acc / l)

    lp = lam_ref[...]
    lam = (jnp.exp(jnp.sum(lp[0:1] * lp[1:2], keepdims=True))
           - jnp.exp(jnp.sum(lp[2:3] * lp[3:4], keepdims=True)) + lambda_init)
    o = outs[0] - lam * outs[1]
    ms = jnp.mean(o * o, axis=1, keepdims=True)
    o_ref[...] = (o * lax.rsqrt(ms + LN_EPS) * (g_ref[...] * (1.0 - lambda_init))).astype(BF16)


def _da_attention(q, k, v, lam_p, g_row, batch, seq, lambda_init):
    nq = seq // BQ
    return pl.pallas_call(
        functools.partial(_da_kernel, lambda_init=lambda_init),
        out_shape=jax.ShapeDtypeStruct(q.shape, BF16),
        grid=(batch, DA_HEADS, nq),
        in_specs=[pl.BlockSpec((BQ, LANES), lambda b, h, i: (b * nq + i, h)),
                  pl.BlockSpec((seq, LANES), lambda b, h, i: (b, h)),
                  pl.BlockSpec((seq, LANES), lambda b, h, i: (b, h)),
                  pl.BlockSpec(lam_p.shape, lambda b, h, i: (0, 0)),
                  pl.BlockSpec((1, LANES), lambda b, h, i: (0, 0))],
        out_specs=pl.BlockSpec((BQ, LANES), lambda b, h, i: (b * nq + i, h)),
        compiler_params=pltpu.CompilerParams(
            dimension_semantics=("arbitrary", "arbitrary", "arbitrary"), vmem_limit_bytes=VMEM_LIMIT),
        name="da_attn",
    )(q, k, v, lam_p, g_row)


def _route_kernel(sb_ref, da_ref, x_ref, wot_ref, wob_ref, g_ref, b_ref, rwt_ref, rb_ref, u_ref,
                  xg_ref, cls_ref, rank_ref, cnt_ref, *, alpha):
    tm = x_ref.shape[0]

    @pl.when(pl.program_id(0) == 0)
    def _():
        cnt_ref[...] = jnp.zeros_like(cnt_ref)

    mix = _dot(sb_ref[...], wot_ref[...]) + _dot(da_ref[...], wob_ref[...])
    x1 = _layer_norm(alpha * x_ref[...] + mix, g_ref[...], b_ref[...])
    xg_ref[:, :D_MODEL] = x1

    logits = _dot_nt(rwt_ref[...], x1, precision=lax.Precision.HIGHEST)
    scores = _sigmoid(logits)
    sel = scores + rb_ref[...]
    grow = lax.broadcasted_iota(jnp.int32, (8, tm), 0)
    a, b, c, d = sel[0:8], sel[8:16], sel[16:24], sel[24:32]
    hi1, lo1 = jnp.maximum(a, b), jnp.minimum(a, b)
    hi2, lo2 = jnp.maximum(c, d), jnp.minimum(c, d)
    top1 = jnp.maximum(hi1, hi2)
    top2 = jnp.maximum(jnp.minimum(hi1, hi2), jnp.maximum(lo1, lo2))
    gscore = jnp.where(grow < N_GROUPS, top1 + top2, -jnp.inf)
    gmax = jnp.max(gscore, axis=0, keepdims=True)
    grp = jnp.min(jnp.where(gscore == gmax, grow, 8), axis=0, keepdims=True)
    gsel = grow == grp

    one = lambda m: jnp.where(m, 1, 0)
    ranks = (one(b > a) + one(c > a) + one(d > a),
             one(a >= b) + one(c > b) + one(d > b),
             one(a >= c) + one(b >= c) + one(d > c),
             one(a >= d) + one(b >= d) + one(c >= d))
    sc = (scores[0:8], scores[8:16], scores[16:24], scores[24:32])
    chosen = [jnp.where(gsel, one(r < TOP_K), 0) for r in ranks]
    picked = [jnp.sum(ch, axis=0, keepdims=True) for ch in chosen]
    num = [jnp.sum(jnp.where(ch == 1, s, 0.0), axis=0, keepdims=True) for ch, s in zip(chosen, sc)]
    den = num[0] + num[1] + num[2] + num[3]
    gate = [n_ / den for n_ in num]
    g_first = jnp.where(picked[0] == 1, gate[0], jnp.where(picked[1] == 1, gate[1], gate[2]))
    g_second = jnp.where(picked[3] == 1, gate[3], jnp.where(picked[2] == 1, gate[2], gate[1]))
    code = picked[0] + 2 * picked[1] + 4 * picked[2] + 8 * picked[3]
    pair = jnp.where(code == 3, 0, jnp.where(code == 5, 1, jnp.where(code == 6, 2,
           jnp.where(code == 9, 3, jnp.where(code == 10, 4, 5)))))
    cls = grp * N_PAIRS + pair

    crow = lax.broadcasted_iota(jnp.int32, (CLASS_ROWS, tm), 0)
    onehot = jnp.where(crow == cls, 1.0, 0.0)
    prefix = _dot(onehot.astype(BF16), u_ref[...])
    before = cnt_ref[:, 0:1]
    rank = jnp.sum(onehot * (prefix + before), axis=0, keepdims=True) - 1.0
    cnt_ref[...] = cnt_ref[...] + jnp.sum(onehot, axis=1, keepdims=True)
    cls_ref[0] = cls
    rank_ref[0] = rank.astype(jnp.int32)

    trow = lax.broadcasted_iota(jnp.int32, (GATE_COLS, tm), 0)
    gates_t = jnp.where(trow == 0, g_first, jnp.where(trow == 1, g_second, 0.0))
    xg_ref[:, D_MODEL:] = gates_t.T


TOP_K = 2


def _proj_route(sb, da, x2d, wo_top, wo_bot, g, b, rw_t, rb_col, u_tri, alpha):
    n = x2d.shape[0]
    tm = TM_ROUTE
    nt = n // tm
    row = lambda i: (i, 0)
    const = lambda i: (0, 0)
    return pl.pallas_call(
        functools.partial(_route_kernel, alpha=alpha),
        out_shape=(jax.ShapeDtypeStruct((n, ROW_W), F32),
                   jax.ShapeDtypeStruct((nt, 1, tm), jnp.int32),
                   jax.ShapeDtypeStruct((nt, 1, tm), jnp.int32),
                   jax.ShapeDtypeStruct((CLASS_ROWS, LANES), F32)),
        grid=(nt,),
        in_specs=[pl.BlockSpec((tm, SB_WIDTH), row), pl.BlockSpec((tm, DA_WIDTH), row),
                  pl.BlockSpec((tm, D_MODEL), row),
                  pl.BlockSpec(wo_top.shape, const), pl.BlockSpec(wo_bot.shape, const),
                  pl.BlockSpec((1, D_MODEL), const), pl.BlockSpec((1, D_MODEL), const),
                  pl.BlockSpec(rw_t.shape, const), pl.BlockSpec(rb_col.shape, const),
                  pl.BlockSpec((tm, tm), const)],
        out_specs=(pl.BlockSpec((tm, ROW_W), row),
                   pl.BlockSpec((1, 1, tm), lambda i: (i, 0, 0)),
                   pl.BlockSpec((1, 1, tm), lambda i: (i, 0, 0)),
                   pl.BlockSpec((CLASS_ROWS, LANES), const)),
        compiler_params=pltpu.CompilerParams(dimension_semantics=("arbitrary",),
                                             vmem_limit_bytes=VMEM_LIMIT),
        name="proj_route",
    )(sb, da, x2d, wo_top, wo_bot, g, b, rw_t, rb_col, u_tri)


def _dispatch_kernel(pos_ref, xg_ref, zeros_ref, xs_ref, sem):
    del zeros_ref
    tm = xg_ref.shape[0]
    base = pl.program_id(0) * tm

    def row_copy(r):
        return pltpu.make_async_copy(xg_ref.at[pl.ds(r, 1), :],
                                     xs_ref.at[pl.ds(pos_ref[base + r], 1), :], sem)

    def issue(r, _):
        row_copy(r).start()
        return 0

    lax.fori_loop(0, tm, issue, 0)
    pltpu.make_async_copy(xg_ref, xs_ref.at[pl.ds(0, tm), :], sem).wait()


def _dispatch(pos, xg, n_rows):
    n = xg.shape[0]
    tm = TM_ROWS
    zeros = jnp.zeros((n_rows, ROW_W), F32)
    return pl.pallas_call(
        _dispatch_kernel,
        out_shape=jax.ShapeDtypeStruct((n_rows, ROW_W), F32),
        grid_spec=pltpu.PrefetchScalarGridSpec(
            num_scalar_prefetch=1,
            grid=(n // tm,),
            in_specs=[pl.BlockSpec((tm, ROW_W), lambda i, pos: (i, 0)),
                      pl.BlockSpec(memory_space=pl.ANY)],
            out_specs=pl.BlockSpec(memory_space=pl.ANY),
            scratch_shapes=[pltpu.SemaphoreType.DMA]),
        input_output_aliases={2: 0},
        compiler_params=pltpu.CompilerParams(dimension_semantics=("arbitrary",),
                                             vmem_limit_bytes=VMEM_LIMIT),
        name="dispatch",
    )(pos, xg, zeros)


def _moe_kernel(e1_ref, e2_ref, nv_ref, xs_ref, wg1, wu1, wd1, wg2, wu2, wd2, ys_ref, *, alpha):
    @pl.when(pl.program_id(0) < nv_ref[0])
    def _():
        x = xs_ref[:, :D_MODEL]
        gates = xs_ref[:, D_MODEL:]
        xb = x.astype(BF16)

        def expert(wg, wu, wd):
            gt = _dot(xb, wg[...])
            h = gt * _sigmoid(gt) * _dot(xb, wu[...])
            return _dot(h.astype(BF16), wd[...])

        ffn = gates[:, 0:1] * expert(wg1, wu1, wd1) + gates[:, 1:2] * expert(wg2, wu2, wd2)
        ys_ref[...] = alpha * x + ffn

    @pl.when(pl.program_id(0) >= nv_ref[0])
    def _():
        ys_ref[...] = jnp.zeros_like(ys_ref)


def _moe(tile_e1, tile_e2, n_valid, xs, wg, wu, wd, alpha):
    n_rows = xs.shape[0]
    n_tiles = n_rows // MOE_TILE
    tile = lambda i, e1, e2, nv: (jnp.minimum(i, nv[0] - 1), 0)
    first = lambda i, e1, e2, nv: (e1[i], 0, 0)
    second = lambda i, e1, e2, nv: (e2[i], 0, 0)
    up_spec = lambda f: pl.BlockSpec((None, D_MODEL, D_FF), f)
    down_spec = lambda f: pl.BlockSpec((None, D_FF, D_MODEL), f)
    return pl.pallas_call(
        functools.partial(_moe_kernel, alpha=alpha),
        out_shape=jax.ShapeDtypeStruct((n_rows, D_MODEL), F32),
        grid_spec=pltpu.PrefetchScalarGridSpec(
            num_scalar_prefetch=3,
            grid=(n_tiles,),
            in_specs=[pl.BlockSpec((MOE_TILE, ROW_W), tile),
                      up_spec(first), up_spec(first), down_spec(first),
                      up_spec(second), up_spec(second), down_spec(second)],
            out_specs=pl.BlockSpec((MOE_TILE, D_MODEL), lambda i, e1, e2, nv: (i, 0))),
        compiler_params=pltpu.CompilerParams(dimension_semantics=("arbitrary",),
                                             vmem_limit_bytes=VMEM_LIMIT),
        name="moe",
    )(tile_e1, tile_e2, n_valid, xs, wg, wu, wd, wg, wu, wd)


def _combine_kernel(pos_ref, ys_ref, p_ref, g_ref, b_ref, wpg_ref, bpg_ref, wple_ref, o_ref, buf, sem):
    tm = o_ref.shape[0]
    base = pl.program_id(0) * tm

    def issue(r, _):
        pltpu.make_async_copy(ys_ref.at[pl.ds(pos_ref[base + r], 1), :],
                              buf.at[pl.ds(r, 1), :], sem).start()
        return 0

    lax.fori_loop(0, tm, issue, 0)
    pltpu.make_async_copy(ys_ref.at[pl.ds(0, tm), :], buf, sem).wait()
    x2 = _layer_norm(buf[...], g_ref[...], b_ref[...])
    gate = _sigmoid(_dot(x2.astype(BF16), wpg_ref[...]) + bpg_ref[...])
    o_ref[...] = x2 + gate * _dot(p_ref[...].astype(BF16), wple_ref[...])


def _combine_ple(pos, ys, p2d, g, b, wpg, bpg, wple):
    n = p2d.shape[0]
    tm = TM_ROWS
    row = lambda i, pos: (i, 0)
    const = lambda i, pos: (0, 0)
    return pl.pallas_call(
        _combine_kernel,
        out_shape=jax.ShapeDtypeStruct((n, D_MODEL), F32),
        grid_spec=pltpu.PrefetchScalarGridSpec(
            num_scalar_prefetch=1,
            grid=(n // tm,),
            in_specs=[pl.BlockSpec(memory_space=pl.ANY),
                      pl.BlockSpec((tm, PLE_DIM), row),
                      pl.BlockSpec((1, D_MODEL), const), pl.BlockSpec((1, D_MODEL), const),
                      pl.BlockSpec(wpg.shape, const), pl.BlockSpec((1, D_MODEL), const),
                      pl.BlockSpec(wple.shape, const)],
            out_specs=pl.BlockSpec((tm, D_MODEL), row),
            scratch_shapes=[pltpu.VMEM((tm, D_MODEL), F32), pltpu.SemaphoreType.DMA]),
        compiler_params=pltpu.CompilerParams(dimension_semantics=("arbitrary",),
                                             vmem_limit_bytes=VMEM_LIMIT),
        name="combine_ple",
    )(pos, ys, p2d, g, b, wpg, bpg, wple)


def _tri(n, inclusive_lower):
    j = np.arange(n)[:, None]
    s = np.arange(n)[None, :]
    return jnp.asarray((j >= s) if inclusive_lower else (j <= s), dtype=BF16)


def kernel(x, p, positions, w_in, w_o, sb_norm_g, da_lambda, da_subln_g, ln1_g, ln1_b, ln2_g, ln2_b,
           router_w, router_b, w_gate, w_up, w_down, w_ple, w_ple_gate, b_ple_gate):
    batch, seq, d = x.shape
    depth = w_in.shape[0]
    n = batch * seq
    alpha = (2 * depth) ** 0.25
    max_tiles = n // MOE_TILE + N_CLASSES
    n_rows = max_tiles * MOE_TILE

    inv_freq = ROPE_THETA ** (-jnp.arange(0, ROPE_DIM, 2, dtype=F32) / ROPE_DIM)
    invf_row = jnp.tile(jnp.concatenate([inv_freq, inv_freq, jnp.zeros((HEAD_DIM - ROPE_DIM,), F32)]),
                        LANES // HEAD_DIM)[None, :]
    cos_t, sin_t = _rope_tables(positions.reshape(n, 1), invf_row)

    src = np.array([[4 * g + l for g in range(N_GROUPS)] for l in range(EXPERTS_PER_GROUP)])
    rw_t = jnp.zeros((EXPERTS_PER_GROUP, 8, d), F32).at[:, :N_GROUPS, :].set(router_w.T[src]).reshape(32, d)
    rb_col = jnp.zeros((EXPERTS_PER_GROUP, 8), F32).at[:, :N_GROUPS].set(router_b[src]).reshape(32, 1)

    u_sb = _tri(BK, True)
    u_rank = _tri(TM_ROUTE, False)
    class_e1 = jnp.asarray(_CLASS_E1)
    class_e2 = jnp.asarray(_CLASS_E2)

    x2d = x.reshape(n, d)
    for i in range(depth):
        lambda_init = 0.8 - 0.6 * math.exp(-0.3 * i)
        sbq, sbk, sbv, daq, dak, dav = _qkv(x2d, w_in[i].astype(BF16), cos_t, sin_t)
        g_sb = jnp.tile(sb_norm_g[i], LANES // HEAD_DIM)[None, :]
        sb = _sb_attention(sbq, sbk, sbv, u_sb, g_sb, batch, seq)
        da = _da_attention(daq, dak, dav, da_lambda[i], da_subln_g[i][None, :], batch, seq, lambda_init)

        wo = w_o[i].astype(BF16)
        xg, cls, rank, counts = _proj_route(sb, da, x2d, wo[:SB_WIDTH], wo[SB_WIDTH:],
                                            ln1_g[i][None, :], ln1_b[i][None, :], rw_t, rb_col, u_rank, alpha)

        cnt = counts[:N_CLASSES, 0].astype(jnp.int32)
        tiles_c = (cnt + MOE_TILE - 1) // MOE_TILE
        tile_end = jnp.cumsum(tiles_c)
        row_off = (tile_end - tiles_c) * MOE_TILE
        pos = row_off[cls.reshape(n)] + rank.reshape(n)
        n_valid = tile_end[-1:]
        tile_ids = jnp.minimum(jnp.arange(max_tiles, dtype=jnp.int32), n_valid[0] - 1)
        tile_cls = jnp.minimum(jnp.searchsorted(tile_end, tile_ids, side="right"), N_CLASSES - 1)

        xs = _dispatch(pos, xg, n_rows)
        ys = _moe(class_e1[tile_cls], class_e2[tile_cls], n_valid.astype(jnp.int32), xs,
                  w_gate[i].astype(BF16), w_up[i].astype(BF16), w_down[i].astype(BF16), alpha)
        x2d = _combine_ple(pos, ys, p[i].reshape(n, PLE_DIM), ln2_g[i][None, :], ln2_b[i][None, :],
                           w_ple_gate[i].astype(BF16), b_ple_gate[i][None, :], w_ple[i].astype(BF16))
    return x2d.reshape(batch, seq, d)
```

```python
import functools
import math

import numpy as np
import jax
import jax.numpy as jnp
from jax import lax
from jax.experimental import pallas as pl
from jax.experimental.pallas import tpu as pltpu

D_MODEL = 1024
HEAD_DIM = 64
SB_WIDTH = 512
DA_WIDTH = 512
DA_HEADS = 4
ROPE_DIM = 16
ROPE_THETA = 500000.0
N_EXPERTS = 16
N_GROUPS = 4
EXPERTS_PER_GROUP = 4
D_FF = 512
PLE_DIM = 256
LN_EPS = 1e-5

LANES = 128
N_PAIRS = 6
N_CLASSES = N_GROUPS * N_PAIRS
CLASS_ROWS = 32
GATE_COLS = LANES
ROW_W = D_MODEL + GATE_COLS

MOE_TILE = 256
SB_BQ = 512
SB_BK = 256
DA_BQ = 512
DA_BK = 512
SB_UNDERFLOW = 104.0
LOG2E = 1.4426950408889634
TM_QKV = 512
TM_ROUTE = 512
TM_ROWS = 512
VMEM_LIMIT = 56 * 1024 * 1024

F32 = jnp.float32
BF16 = jnp.bfloat16

_PAIR_LOCAL = [(0, 1), (0, 2), (1, 2), (0, 3), (1, 3), (2, 3)]
_CLASS_E1 = np.array([g * 4 + _PAIR_LOCAL[p][0] for g in range(N_GROUPS) for p in range(N_PAIRS)], np.int32)
_CLASS_E2 = np.array([g * 4 + _PAIR_LOCAL[p][1] for g in range(N_GROUPS) for p in range(N_PAIRS)], np.int32)


def _dot(a, b):
    return jnp.dot(a, b, preferred_element_type=F32)


def _dot_nt(a, b, precision=None):
    return lax.dot_general(a, b, (((1,), (1,)), ((), ())), precision=precision,
                           preferred_element_type=F32)


def _sigmoid(x):
    return 1.0 / (1.0 + jnp.exp(-x))


def _layer_norm(y, g, b):
    mu = jnp.mean(y, axis=-1, keepdims=True)
    yc = y - mu
    var = jnp.mean(yc * yc, axis=-1, keepdims=True)
    return yc * lax.rsqrt(var + LN_EPS) * g + b


def _rope_table_kernel(pos_ref, invf_ref, cos_ref, sin_ref):
    ang = pos_ref[...].astype(F32) * invf_ref[...]
    cos_ref[...] = jnp.cos(ang)
    sin_ref[...] = jnp.sin(ang)


def _rope_tables(pos_col, invf_row):
    n = pos_col.shape[0]
    tm = 1024
    return pl.pallas_call(
        _rope_table_kernel,
        out_shape=(jax.ShapeDtypeStruct((n, LANES), F32), jax.ShapeDtypeStruct((n, LANES), F32)),
        grid=(n // tm,),
        in_specs=[pl.BlockSpec((tm, 1), lambda i: (i, 0)),
                  pl.BlockSpec((1, LANES), lambda i: (0, 0))],
        out_specs=(pl.BlockSpec((tm, LANES), lambda i: (i, 0)),
                   pl.BlockSpec((tm, LANES), lambda i: (i, 0))),
        compiler_params=pltpu.CompilerParams(dimension_semantics=("arbitrary",)),
        name="rope_tables",
    )(pos_col, invf_row)


def _qkv_kernel(x_ref, w_ref, cos_ref, sin_ref, sbq, sbk, sbv, daq, dak, dav):
    xb = x_ref[...].astype(BF16)
    lane = lax.broadcasted_iota(jnp.int32, (1, LANES), 1) % HEAD_DIM
    half = ROPE_DIM // 2
    cosr = jnp.where(lane < ROPE_DIM, cos_ref[...], 1.0)
    sin_lo = jnp.where(lane < half, -sin_ref[...], 0.0)
    sin_hi = jnp.where((lane >= half) & (lane < ROPE_DIM), sin_ref[...], 0.0)
    scale = HEAD_DIM ** -0.5
    outs = (sbq, sbk, sbv, daq, dak, dav)
    for c, o_ref in enumerate(outs):
        hc = _dot(xb, w_ref[:, c * SB_WIDTH:(c + 1) * SB_WIDTH])
        for blk in range(SB_WIDTH // LANES):
            h = hc[:, blk * LANES:(blk + 1) * LANES]
            if c in (3, 4):
                h = (h * cosr + pltpu.roll(h, LANES - half, 1) * sin_lo
                     + pltpu.roll(h, half, 1) * sin_hi)
            if c in (0, 3):
                h = h * scale
            o_ref[:, blk * LANES:(blk + 1) * LANES] = h.astype(BF16)


def _qkv(x2d, w_bf16, cos_t, sin_t):
    n = x2d.shape[0]
    tm = TM_QKV
    row = lambda i: (i, 0)
    out_sd = jax.ShapeDtypeStruct((n, SB_WIDTH), BF16)
    return pl.pallas_call(
        _qkv_kernel,
        out_shape=(out_sd,) * 6,
        grid=(n // tm,),
        in_specs=[pl.BlockSpec((tm, D_MODEL), row),
                  pl.BlockSpec(w_bf16.shape, lambda i: (0, 0)),
                  pl.BlockSpec((tm, LANES), row),
                  pl.BlockSpec((tm, LANES), row)],
        out_specs=(pl.BlockSpec((tm, SB_WIDTH), row),) * 6,
        compiler_params=pltpu.CompilerParams(dimension_semantics=("arbitrary",),
                                             vmem_limit_bytes=VMEM_LIMIT),
        name="qkv",
    )(x2d, w_bf16, cos_t, sin_t)


def _stack_halves(q):
    first = lax.broadcasted_iota(jnp.int32, (1, LANES), 1) < HEAD_DIM
    zero = jnp.zeros_like(q)
    return jnp.concatenate([jnp.where(first, q, zero), jnp.where(first, zero, q)], axis=0)


def _local_positions(rows, bq, bk):
    r = lax.broadcasted_iota(jnp.int32, (rows, bk), 0)
    return jnp.where(r >= bq, r - bq, r), lax.broadcasted_iota(jnp.int32, (rows, bk), 1)


def _sb_kernel(q_ref, k_ref, v_ref, u_ref, g_ref, o_ref):
    qi = pl.program_id(2)
    qs = _stack_halves(q_ref[...])
    rows = 2 * SB_BQ
    t_local, s_local = _local_positions(rows, SB_BQ, SB_BK)
    u = u_ref[...]

    def block(kb, carry, acc, mask):
        start = pl.multiple_of(kb * SB_BK, SB_BK)
        ks = k_ref[pl.ds(start, SB_BK), :]
        vs = v_ref[pl.ds(start, SB_BK), :]
        z = _dot_nt(qs, ks)
        sp = jnp.maximum(z, 0.0) + jnp.log(1.0 + jnp.exp2(jnp.abs(z) * (-LOG2E)))
        if mask is not None:
            sp = jnp.where(mask, sp, 0.0)
        sp_hi = sp.astype(BF16)
        sp_lo = (sp - sp_hi.astype(F32)).astype(BF16)
        tail = _dot(sp_hi, u) + _dot(sp_lo, u)
        a = jnp.exp(z - tail - carry)
        if mask is not None:
            a = jnp.where(mask, a, 0.0)
        acc = acc + _dot(a.astype(BF16), vs)
        carry = carry + jnp.sum(sp, axis=1, keepdims=True)
        return carry, acc

    carry, acc = jnp.zeros((rows, 1), F32), jnp.zeros((rows, LANES), F32)
    per_q = SB_BQ // SB_BK
    for j in reversed(range(per_q)):
        carry, acc = block(qi * per_q + j, carry, acc, s_local + j * SB_BK < t_local)

    def more(st):
        return (st[0] >= 0) & (jnp.min(st[1]) < SB_UNDERFLOW)

    def step(st):
        carry, acc = block(st[0], st[1], st[2], None)
        return st[0] - 1, carry, acc

    _, carry, acc = lax.while_loop(more, step, (qi * per_q - 1, carry, acc))

    first = lax.broadcasted_iota(jnp.int32, (1, LANES), 1) < HEAD_DIM
    o = jnp.where(first, acc[:SB_BQ], acc[SB_BQ:])
    sq = o * o
    ss0 = jnp.sum(jnp.where(first, sq, 0.0), axis=1, keepdims=True)
    ss1 = jnp.sum(jnp.where(first, 0.0, sq), axis=1, keepdims=True)
    ms = jnp.where(first, ss0, ss1) * (1.0 / HEAD_DIM)
    o_ref[...] = (o * lax.rsqrt(ms + LN_EPS) * g_ref[...]).astype(BF16)


def _sb_attention(q, k, v, u_tri, g_row, batch, seq):
    nq = seq // SB_BQ
    n_pairs = SB_WIDTH // LANES
    return pl.pallas_call(
        _sb_kernel,
        out_shape=jax.ShapeDtypeStruct(q.shape, BF16),
        grid=(batch, n_pairs, nq),
        in_specs=[pl.BlockSpec((SB_BQ, LANES), lambda b, h, i: (b * nq + i, h)),
                  pl.BlockSpec((seq, LANES), lambda b, h, i: (b, h)),
                  pl.BlockSpec((seq, LANES), lambda b, h, i: (b, h)),
                  pl.BlockSpec((SB_BK, SB_BK), lambda b, h, i: (0, 0)),
                  pl.BlockSpec((1, LANES), lambda b, h, i: (0, 0))],
        out_specs=pl.BlockSpec((SB_BQ, LANES), lambda b, h, i: (b * nq + i, h)),
        compiler_params=pltpu.CompilerParams(
            dimension_semantics=("arbitrary", "arbitrary", "arbitrary"), vmem_limit_bytes=VMEM_LIMIT),
        name="sb_attn",
    )(q, k, v, u_tri, g_row)


def _da_kernel(q_ref, k_ref, v_ref, lam_ref, g_ref, o_ref, *, lambda_init):
    qi = pl.program_id(2)
    qs = _stack_halves(q_ref[...])
    rows = 2 * DA_BQ
    t_local, s_local = _local_positions(rows, DA_BQ, DA_BK)
    causal = s_local <= t_local

    def block(kb, m, l, acc, mask):
        start = pl.multiple_of(kb * DA_BK, DA_BK)
        ks = k_ref[pl.ds(start, DA_BK), :]
        vs = v_ref[pl.ds(start, DA_BK), :]
        s = _dot_nt(qs, ks)
        if mask is not None:
            s = jnp.where(mask, s, -jnp.inf)
        m_blk = jnp.max(s, axis=1, keepdims=True)
        m_new = m_blk if m is None else jnp.maximum(m, m_blk)
        p = jnp.exp(s - m_new)
        pv = _dot(p.astype(BF16), vs)
        ps = jnp.sum(p, axis=1, keepdims=True)
        if m is None:
            return m_new, ps, pv
        alpha = jnp.exp(m - m_new)
        return m_new, alpha * l + ps, alpha * acc + pv

    m, l, acc = block(qi, None, None, None, causal)
    m, l, acc = lax.fori_loop(0, qi, lambda kb, st: block(kb, st[0], st[1], st[2], None), (m, l, acc))
    o_maps = acc / l

    lp = lam_ref[...]
    lam = (jnp.exp(jnp.sum(lp[0:1] * lp[1:2], keepdims=True))
           - jnp.exp(jnp.sum(lp[2:3] * lp[3:4], keepdims=True)) + lambda_init)
    o = o_maps[:DA_BQ] - lam * o_maps[DA_BQ:]
    ms = jnp.mean(o * o, axis=1, keepdims=True)
    o_ref[...] = (o * lax.rsqrt(ms + LN_EPS) * (g_ref[...] * (1.0 - lambda_init))).astype(BF16)


def _da_attention(q, k, v, lam_p, g_row, batch, seq, lambda_init):
    nq = seq // DA_BQ
    return pl.pallas_call(
        functools.partial(_da_kernel, lambda_init=lambda_init),
        out_shape=jax.ShapeDtypeStruct(q.shape, BF16),
        grid=(batch, DA_HEADS, nq),
        in_specs=[pl.BlockSpec((DA_BQ, LANES), lambda b, h, i: (b * nq + i, h)),
                  pl.BlockSpec((seq, LANES), lambda b, h, i: (b, h)),
                  pl.BlockSpec((seq, LANES), lambda b, h, i: (b, h)),
                  pl.BlockSpec(lam_p.shape, lambda b, h, i: (0, 0)),
                  pl.BlockSpec((1, LANES), lambda b, h, i: (0, 0))],
        out_specs=pl.BlockSpec((DA_BQ, LANES), lambda b, h, i: (b * nq + i, h)),
        compiler_params=pltpu.CompilerParams(
            dimension_semantics=("arbitrary", "arbitrary", "arbitrary"), vmem_limit_bytes=VMEM_LIMIT),
        name="da_attn",
    )(q, k, v, lam_p, g_row)


def _route_kernel(sb_ref, da_ref, x_ref, wot_ref, wob_ref, g_ref, b_ref, rwt_ref, rb_ref, u_ref,
                  xg_ref, cls_ref, rank_ref, cnt_ref, *, alpha):
    tm = x_ref.shape[0]

    @pl.when(pl.program_id(0) == 0)
    def _():
        cnt_ref[...] = jnp.zeros_like(cnt_ref)

    mix = _dot(sb_ref[...], wot_ref[...]) + _dot(da_ref[...], wob_ref[...])
    x1 = _layer_norm(alpha * x_ref[...] + mix, g_ref[...], b_ref[...])
    xg_ref[:, :D_MODEL] = x1

    logits = _dot_nt(rwt_ref[...], x1, precision=lax.Precision.HIGHEST)
    scores = _sigmoid(logits)
    sel = scores + rb_ref[...]
    grow = lax.broadcasted_iota(jnp.int32, (8, tm), 0)
    a, b, c, d = sel[0:8], sel[8:16], sel[16:24], sel[24:32]
    hi1, lo1 = jnp.maximum(a, b), jnp.minimum(a, b)
    hi2, lo2 = jnp.maximum(c, d), jnp.minimum(c, d)
    top1 = jnp.maximum(hi1, hi2)
    top2 = jnp.maximum(jnp.minimum(hi1, hi2), jnp.maximum(lo1, lo2))
    gscore = jnp.where(grow < N_GROUPS, top1 + top2, -jnp.inf)
    gmax = jnp.max(gscore, axis=0, keepdims=True)
    grp = jnp.min(jnp.where(gscore == gmax, grow, 8), axis=0, keepdims=True)
    gsel = grow == grp

    one = lambda m: jnp.where(m, 1, 0)
    ranks = (one(b > a) + one(c > a) + one(d > a),
             one(a >= b) + one(c > b) + one(d > b),
             one(a >= c) + one(b >= c) + one(d > c),
             one(a >= d) + one(b >= d) + one(c >= d))
    sc = (scores[0:8], scores[8:16], scores[16:24], scores[24:32])
    chosen = [jnp.where(gsel, one(r < TOP_K), 0) for r in ranks]
    picked = [jnp.sum(ch, axis=0, keepdims=True) for ch in chosen]
    num = [jnp.sum(jnp.where(ch == 1, s, 0.0), axis=0, keepdims=True) for ch, s in zip(chosen, sc)]
    den = num[0] + num[1] + num[2] + num[3]
    gate = [n_ / den for n_ in num]
    g_first = jnp.where(picked[0] == 1, gate[0], jnp.where(picked[1] == 1, gate[1], gate[2]))
    g_second = jnp.where(picked[3] == 1, gate[3], jnp.where(picked[2] == 1, gate[2], gate[1]))
    code = picked[0] + 2 * picked[1] + 4 * picked[2] + 8 * picked[3]
    pair = jnp.where(code == 3, 0, jnp.where(code == 5, 1, jnp.where(code == 6, 2,
           jnp.where(code == 9, 3, jnp.where(code == 10, 4, 5)))))
    cls = grp * N_PAIRS + pair

    crow = lax.broadcasted_iota(jnp.int32, (CLASS_ROWS, tm), 0)
    onehot = jnp.where(crow == cls, 1.0, 0.0)
    prefix = _dot(onehot.astype(BF16), u_ref[...])
    before = cnt_ref[:, 0:1]
    rank = jnp.sum(onehot * (prefix + before), axis=0, keepdims=True) - 1.0
    cnt_ref[...] = cnt_ref[...] + jnp.sum(onehot, axis=1, keepdims=True)
    cls_ref[0] = cls
    rank_ref[0] = rank.astype(jnp.int32)

    trow = lax.broadcasted_iota(jnp.int32, (GATE_COLS, tm), 0)
    gates_t = jnp.where(trow == 0, g_first, jnp.where(trow == 1, g_second, 0.0))
    xg_ref[:, D_MODEL:] = gates_t.T


TOP_K = 2


def _proj_route(sb, da, x2d, wo_top, wo_bot, g, b, rw_t, rb_col, u_tri, alpha):
    n = x2d.shape[0]
    tm = TM_ROUTE
    nt = n // tm
    row = lambda i: (i, 0)
    const = lambda i: (0, 0)
    return pl.pallas_call(
        functools.partial(_route_kernel, alpha=alpha),
        out_shape=(jax.ShapeDtypeStruct((n, ROW_W), F32),
                   jax.ShapeDtypeStruct((nt, 1, tm), jnp.int32),
                   jax.ShapeDtypeStruct((nt, 1, tm), jnp.int32),
                   jax.ShapeDtypeStruct((CLASS_ROWS, LANES), F32)),
        grid=(nt,),
        in_specs=[pl.BlockSpec((tm, SB_WIDTH), row), pl.BlockSpec((tm, DA_WIDTH), row),
                  pl.BlockSpec((tm, D_MODEL), row),
                  pl.BlockSpec(wo_top.shape, const), pl.BlockSpec(wo_bot.shape, const),
                  pl.BlockSpec((1, D_MODEL), const), pl.BlockSpec((1, D_MODEL), const),
                  pl.BlockSpec(rw_t.shape, const), pl.BlockSpec(rb_col.shape, const),
                  pl.BlockSpec((tm, tm), const)],
        out_specs=(pl.BlockSpec((tm, ROW_W), row),
                   pl.BlockSpec((1, 1, tm), lambda i: (i, 0, 0)),
                   pl.BlockSpec((1, 1, tm), lambda i: (i, 0, 0)),
                   pl.BlockSpec((CLASS_ROWS, LANES), const)),
        compiler_params=pltpu.CompilerParams(dimension_semantics=("arbitrary",),
                                             vmem_limit_bytes=VMEM_LIMIT),
        name="proj_route",
    )(sb, da, x2d, wo_top, wo_bot, g, b, rw_t, rb_col, u_tri)


def _dispatch_kernel(pos_ref, xg_ref, zeros_ref, xs_ref, sem):
    del zeros_ref
    tm = xg_ref.shape[0]
    base = pl.program_id(0) * tm

    def row_copy(r):
        return pltpu.make_async_copy(xg_ref.at[pl.ds(r, 1), :],
                                     xs_ref.at[pl.ds(pos_ref[base + r], 1), :], sem)

    def issue(r, _):
        row_copy(r).start()
        return 0

    lax.fori_loop(0, tm, issue, 0)
    pltpu.make_async_copy(xg_ref, xs_ref.at[pl.ds(0, tm), :], sem).wait()


def _dispatch(pos, xg, n_rows):
    n = xg.shape[0]
    tm = TM_ROWS
    zeros = jnp.zeros((n_rows, ROW_W), F32)
    return pl.pallas_call(
        _dispatch_kernel,
        out_shape=jax.ShapeDtypeStruct((n_rows, ROW_W), F32),
        grid_spec=pltpu.PrefetchScalarGridSpec(
            num_scalar_prefetch=1,
            grid=(n // tm,),
            in_specs=[pl.BlockSpec((tm, ROW_W), lambda i, pos: (i, 0)),
                      pl.BlockSpec(memory_space=pl.ANY)],
            out_specs=pl.BlockSpec(memory_space=pl.ANY),
            scratch_shapes=[pltpu.SemaphoreType.DMA]),
        input_output_aliases={2: 0},
        compiler_params=pltpu.CompilerParams(dimension_semantics=("arbitrary",),
                                             vmem_limit_bytes=VMEM_LIMIT),
        name="dispatch",
    )(pos, xg, zeros)


def _moe_kernel(e1_ref, e2_ref, nv_ref, xs_ref, wg1, wu1, wd1, wg2, wu2, wd2, ys_ref, *, alpha):
    @pl.when(pl.program_id(0) < nv_ref[0])
    def _():
        x = xs_ref[:, :D_MODEL]
        gates = xs_ref[:, D_MODEL:]
        xb = x.astype(BF16)

        def expert(wg, wu, wd):
            gt = _dot(xb, wg[...])
            h = gt * _sigmoid(gt) * _dot(xb, wu[...])
            return _dot(h.astype(BF16), wd[...])

        ffn = gates[:, 0:1] * expert(wg1, wu1, wd1) + gates[:, 1:2] * expert(wg2, wu2, wd2)
        ys_ref[...] = alpha * x + ffn

    @pl.when(pl.program_id(0) >= nv_ref[0])
    def _():
        ys_ref[...] = jnp.zeros_like(ys_ref)


def _moe(tile_e1, tile_e2, n_valid, xs, wg, wu, wd, alpha):
    n_rows = xs.shape[0]
    n_tiles = n_rows // MOE_TILE
    tile = lambda i, e1, e2, nv: (jnp.minimum(i, nv[0] - 1), 0)
    first = lambda i, e1, e2, nv: (e1[i], 0, 0)
    second = lambda i, e1, e2, nv: (e2[i], 0, 0)
    up_spec = lambda f: pl.BlockSpec((None, D_MODEL, D_FF), f)
    down_spec = lambda f: pl.BlockSpec((None, D_FF, D_MODEL), f)
    return pl.pallas_call(
        functools.partial(_moe_kernel, alpha=alpha),
        out_shape=jax.ShapeDtypeStruct((n_rows, D_MODEL), F32),
        grid_spec=pltpu.PrefetchScalarGridSpec(
            num_scalar_prefetch=3,
            grid=(n_tiles,),
            in_specs=[pl.BlockSpec((MOE_TILE, ROW_W), tile),
                      up_spec(first), up_spec(first), down_spec(first),
                      up_spec(second), up_spec(second), down_spec(second)],
            out_specs=pl.BlockSpec((MOE_TILE, D_MODEL), lambda i, e1, e2, nv: (i, 0))),
        compiler_params=pltpu.CompilerParams(dimension_semantics=("arbitrary",),
                                             vmem_limit_bytes=VMEM_LIMIT),
        name="moe",
    )(tile_e1, tile_e2, n_valid, xs, wg, wu, wd, wg, wu, wd)


def _combine_kernel(pos_ref, ys_ref, p_ref, g_ref, b_ref, wpg_ref, bpg_ref, wple_ref, o_ref, buf, sem):
    tm = o_ref.shape[0]
    base = pl.program_id(0) * tm

    def issue(r, _):
        pltpu.make_async_copy(ys_ref.at[pl.ds(pos_ref[base + r], 1), :],
                              buf.at[pl.ds(r, 1), :], sem).start()
        return 0

    lax.fori_loop(0, tm, issue, 0)
    pltpu.make_async_copy(ys_ref.at[pl.ds(0, tm), :], buf, sem).wait()
    x2 = _layer_norm(buf[...], g_ref[...], b_ref[...])
    gate = _sigmoid(_dot(x2.astype(BF16), wpg_ref[...]) + bpg_ref[...])
    o_ref[...] = x2 + gate * _dot(p_ref[...].astype(BF16), wple_ref[...])


def _combine_ple(pos, ys, p2d, g, b, wpg, bpg, wple):
    n = p2d.shape[0]
    tm = TM_ROWS
    row = lambda i, pos: (i, 0)
    const = lambda i, pos: (0, 0)
    return pl.pallas_call(
        _combine_kernel,
        out_shape=jax.ShapeDtypeStruct((n, D_MODEL), F32),
        grid_spec=pltpu.PrefetchScalarGridSpec(
            num_scalar_prefetch=1,
            grid=(n // tm,),
            in_specs=[pl.BlockSpec(memory_space=pl.ANY),
                      pl.BlockSpec((tm, PLE_DIM), row),
                      pl.BlockSpec((1, D_MODEL), const), pl.BlockSpec((1, D_MODEL), const),
                      pl.BlockSpec(wpg.shape, const), pl.BlockSpec((1, D_MODEL), const),
                      pl.BlockSpec(wple.shape, const)],
            out_specs=pl.BlockSpec((tm, D_MODEL), row),
            scratch_shapes=[pltpu.VMEM((tm, D_MODEL), F32), pltpu.SemaphoreType.DMA]),
        compiler_params=pltpu.CompilerParams(dimension_semantics=("arbitrary",),
                                             vmem_limit_bytes=VMEM_LIMIT),
        name="combine_ple",
    )(pos, ys, p2d, g, b, wpg, bpg, wple)


def _tri(n, inclusive_lower):
    j = np.arange(n)[:, None]
    s = np.arange(n)[None, :]
    return jnp.asarray((j >= s) if inclusive_lower else (j <= s), dtype=BF16)


def kernel(x, p, positions, w_in, w_o, sb_norm_g, da_lambda, da_subln_g, ln1_g, ln1_b, ln2_g, ln2_b,
           router_w, router_b, w_gate, w_up, w_down, w_ple, w_ple_gate, b_ple_gate):
    batch, seq, d = x.shape
    depth = w_in.shape[0]
    n = batch * seq
    alpha = (2 * depth) ** 0.25
    max_tiles = n // MOE_TILE + N_CLASSES
    n_rows = max_tiles * MOE_TILE

    inv_freq = ROPE_THETA ** (-jnp.arange(0, ROPE_DIM, 2, dtype=F32) / ROPE_DIM)
    invf_row = jnp.tile(jnp.concatenate([inv_freq, inv_freq, jnp.zeros((HEAD_DIM - ROPE_DIM,), F32)]),
                        LANES // HEAD_DIM)[None, :]
    cos_t, sin_t = _rope_tables(positions.reshape(n, 1), invf_row)

    src = np.array([[4 * g + l for g in range(N_GROUPS)] for l in range(EXPERTS_PER_GROUP)])
    rw_t = jnp.zeros((EXPERTS_PER_GROUP, 8, d), F32).at[:, :N_GROUPS, :].set(router_w.T[src]).reshape(32, d)
    rb_col = jnp.zeros((EXPERTS_PER_GROUP, 8), F32).at[:, :N_GROUPS].set(router_b[src]).reshape(32, 1)

    u_sb = _tri(SB_BK, True)
    u_rank = _tri(TM_ROUTE, False)
    class_e1 = jnp.asarray(_CLASS_E1)
    class_e2 = jnp.asarray(_CLASS_E2)

    x2d = x.reshape(n, d)
    for i in range(depth):
        lambda_init = 0.8 - 0.6 * math.exp(-0.3 * i)
        sbq, sbk, sbv, daq, dak, dav = _qkv(x2d, w_in[i].astype(BF16), cos_t, sin_t)
        g_sb = jnp.tile(sb_norm_g[i], LANES // HEAD_DIM)[None, :]
        sb = _sb_attention(sbq, sbk, sbv, u_sb, g_sb, batch, seq)
        da = _da_attention(daq, dak, dav, da_lambda[i], da_subln_g[i][None, :], batch, seq, lambda_init)

        wo = w_o[i].astype(BF16)
        xg, cls, rank, counts = _proj_route(sb, da, x2d, wo[:SB_WIDTH], wo[SB_WIDTH:],
                                            ln1_g[i][None, :], ln1_b[i][None, :], rw_t, rb_col, u_rank, alpha)

        cnt = counts[:N_CLASSES, 0].astype(jnp.int32)
        tiles_c = (cnt + MOE_TILE - 1) // MOE_TILE
        tile_end = jnp.cumsum(tiles_c)
        row_off = (tile_end - tiles_c) * MOE_TILE
        pos = row_off[cls.reshape(n)] + rank.reshape(n)
        n_valid = tile_end[-1:]
        tile_ids = jnp.minimum(jnp.arange(max_tiles, dtype=jnp.int32), n_valid[0] - 1)
        tile_cls = jnp.minimum(jnp.sum((tile_ids[:, None] >= tile_end[None, :]).astype(jnp.int32), axis=1),
                               N_CLASSES - 1)

        xs = _dispatch(pos, xg, n_rows)
        ys = _moe(class_e1[tile_cls], class_e2[tile_cls], n_valid.astype(jnp.int32), xs,
                  w_gate[i].astype(BF16), w_up[i].astype(BF16), w_down[i].astype(BF16), alpha)
        x2d = _combine_ple(pos, ys, p[i].reshape(n, PLE_DIM), ln2_g[i][None, :], ln2_b[i][None, :],
                           w_ple_gate[i].astype(BF16), b_ple_gate[i][None, :], w_ple[i].astype(BF16))
    return x2d.reshape(batch, seq, d)
```

```python
import functools
import math

import numpy as np
import jax
import jax.numpy as jnp
from jax import lax
from jax.experimental import pallas as pl
from jax.experimental.pallas import tpu as pltpu

D_MODEL = 1024
HEAD_DIM = 64
SB_WIDTH = 512
DA_WIDTH = 512
DA_HEADS = 4
ROPE_DIM = 16
ROPE_THETA = 500000.0
N_EXPERTS = 16
N_GROUPS = 4
EXPERTS_PER_GROUP = 4
D_FF = 512
PLE_DIM = 256
LN_EPS = 1e-5

LANES = 128
N_PAIRS = 6
N_CLASSES = N_GROUPS * N_PAIRS
CLASS_ROWS = 32
ROUTER_ROWS = 4 * 8
ROW_UNROLL = 8
GATE_COLS = LANES
ROW_W = D_MODEL + GATE_COLS

MOE_TILE = 256
SB_BQ = 512
SB_BK = 256
DA_BQ = 512
DA_BK = 512
SB_UNDERFLOW = 104.0
LOG2E = 1.4426950408889634
TM_QKV = 512
TM_ROUTE = 512
TM_DISPATCH = 2048
TM_ROWS = 512
VMEM_LIMIT = 56 * 1024 * 1024

F32 = jnp.float32
BF16 = jnp.bfloat16

_PAIR_LOCAL = [(0, 1), (0, 2), (1, 2), (0, 3), (1, 3), (2, 3)]
_CLASS_E1 = np.array([g * 4 + _PAIR_LOCAL[p][0] for g in range(N_GROUPS) for p in range(N_PAIRS)], np.int32)
_CLASS_E2 = np.array([g * 4 + _PAIR_LOCAL[p][1] for g in range(N_GROUPS) for p in range(N_PAIRS)], np.int32)


def _dot(a, b):
    return jnp.dot(a, b, preferred_element_type=F32)


def _dot_nt(a, b, precision=None):
    return lax.dot_general(a, b, (((1,), (1,)), ((), ())), precision=precision,
                           preferred_element_type=F32)


def _sigmoid(x):
    return 1.0 / (1.0 + jnp.exp(-x))


def _layer_norm(y, g, b):
    mu = jnp.mean(y, axis=-1, keepdims=True)
    yc = y - mu
    var = jnp.mean(yc * yc, axis=-1, keepdims=True)
    return yc * lax.rsqrt(var + LN_EPS) * g + b


def _rope_table_kernel(pos_ref, invf_ref, cos_ref, sin_ref):
    ang = pos_ref[...].astype(F32) * invf_ref[...]
    cos_ref[...] = jnp.cos(ang)
    sin_ref[...] = jnp.sin(ang)


def _rope_tables(pos_col, invf_row):
    n = pos_col.shape[0]
    tm = 1024
    return pl.pallas_call(
        _rope_table_kernel,
        out_shape=(jax.ShapeDtypeStruct((n, LANES), F32), jax.ShapeDtypeStruct((n, LANES), F32)),
        grid=(n // tm,),
        in_specs=[pl.BlockSpec((tm, 1), lambda i: (i, 0)),
                  pl.BlockSpec((1, LANES), lambda i: (0, 0))],
        out_specs=(pl.BlockSpec((tm, LANES), lambda i: (i, 0)),
                   pl.BlockSpec((tm, LANES), lambda i: (i, 0))),
        compiler_params=pltpu.CompilerParams(dimension_semantics=("arbitrary",)),
        name="rope_tables",
    )(pos_col, invf_row)


def _qkv_kernel(x_ref, w_ref, cos_ref, sin_ref, sbq, sbk, sbv, daq, dak, dav):
    xb = x_ref[...].astype(BF16)
    lane = lax.broadcasted_iota(jnp.int32, (1, LANES), 1) % HEAD_DIM
    half = ROPE_DIM // 2
    cosr = jnp.where(lane < ROPE_DIM, cos_ref[...], 1.0)
    sin_lo = jnp.where(lane < half, -sin_ref[...], 0.0)
    sin_hi = jnp.where((lane >= half) & (lane < ROPE_DIM), sin_ref[...], 0.0)
    scale = HEAD_DIM ** -0.5
    outs = (sbq, sbk, sbv, daq, dak, dav)
    for c, o_ref in enumerate(outs):
        hc = _dot(xb, w_ref[:, c * SB_WIDTH:(c + 1) * SB_WIDTH])
        for blk in range(SB_WIDTH // LANES):
            h = hc[:, blk * LANES:(blk + 1) * LANES]
            if c in (3, 4):
                h = (h * cosr + pltpu.roll(h, LANES - half, 1) * sin_lo
                     + pltpu.roll(h, half, 1) * sin_hi)
            if c in (0, 3):
                h = h * scale
            o_ref[:, blk * LANES:(blk + 1) * LANES] = h.astype(BF16)


def _qkv(x2d, w_bf16, cos_t, sin_t, layer):
    n = x2d.shape[0]
    tm = TM_QKV
    row = lambda i: (i, 0)
    out_sd = jax.ShapeDtypeStruct((n, SB_WIDTH), BF16)
    return pl.pallas_call(
        _qkv_kernel,
        out_shape=(out_sd,) * 6,
        grid=(n // tm,),
        in_specs=[pl.BlockSpec((tm, D_MODEL), row),
                  pl.BlockSpec((None,) + w_bf16.shape[1:], lambda i: (layer, 0, 0)),
                  pl.BlockSpec((tm, LANES), row),
                  pl.BlockSpec((tm, LANES), row)],
        out_specs=(pl.BlockSpec((tm, SB_WIDTH), row),) * 6,
        compiler_params=pltpu.CompilerParams(dimension_semantics=("arbitrary",),
                                             vmem_limit_bytes=VMEM_LIMIT),
        name="qkv",
    )(x2d, w_bf16, cos_t, sin_t)


def _stack_halves(q):
    first = lax.broadcasted_iota(jnp.int32, (1, LANES), 1) < HEAD_DIM
    zero = jnp.zeros_like(q)
    return jnp.concatenate([jnp.where(first, q, zero), jnp.where(first, zero, q)], axis=0)


def _local_positions(rows, bq, bk):
    r = lax.broadcasted_iota(jnp.int32, (rows, bk), 0)
    return jnp.where(r >= bq, r - bq, r), lax.broadcasted_iota(jnp.int32, (rows, bk), 1)


def _sb_kernel(q_ref, k_ref, v_ref, u_ref, g_ref, o_ref):
    qi = pl.program_id(2)
    qs = _stack_halves(q_ref[...])
    rows = 2 * SB_BQ
    t_local, s_local = _local_positions(rows, SB_BQ, SB_BK)
    u = u_ref[...]

    def block(qm, kb, carry, acc, mask):
        start = pl.multiple_of(kb * SB_BK, SB_BK)
        ks = k_ref[pl.ds(start, SB_BK), :]
        vs = v_ref[pl.ds(start, SB_BK), :]
        z = _dot_nt(qm, ks)
        sp = jnp.maximum(z, 0.0) + jnp.log(1.0 + jnp.exp2(jnp.abs(z) * (-LOG2E)))
        if mask is not None:
            sp = jnp.where(mask, sp, 0.0)
        sp_hi = sp.astype(BF16)
        sp_lo = (sp - sp_hi.astype(F32)).astype(BF16)
        tail = _dot(sp_hi, u) + _dot(sp_lo, u)
        a = jnp.exp(z - tail - carry)
        if mask is not None:
            a = jnp.where(mask, a, 0.0)
        acc = acc + _dot(a.astype(BF16), vs)
        carry = carry + jnp.sum(sp, axis=1, keepdims=True)
        return carry, acc

    carry, acc = jnp.zeros((rows, 1), F32), jnp.zeros((rows, LANES), F32)
    per_q = SB_BQ // SB_BK
    for j in reversed(range(per_q)):
        carry, acc = block(qs, qi * per_q + j, carry, acc, s_local + j * SB_BK < t_local)

    def more(st):
        return (st[0] >= 0) & (jnp.min(st[1]) < SB_UNDERFLOW)

    def step(st):
        carry, acc = block(qs, st[0], st[1], st[2], None)
        return st[0] - 1, carry, acc

    _, carry, acc = lax.while_loop(more, step, (qi * per_q - 1, carry, acc))

    first = lax.broadcasted_iota(jnp.int32, (1, LANES), 1) < HEAD_DIM
    o = jnp.where(first, acc[:SB_BQ], acc[SB_BQ:])
    sq = o * o
    ss0 = jnp.sum(jnp.where(first, sq, 0.0), axis=1, keepdims=True)
    ss1 = jnp.sum(jnp.where(first, 0.0, sq), axis=1, keepdims=True)
    ms = jnp.where(first, ss0, ss1) * (1.0 / HEAD_DIM)
    o_ref[...] = (o * lax.rsqrt(ms + LN_EPS) * g_ref[...]).astype(BF16)


def _sb_attention(q, k, v, u_tri, g_row, batch, seq):
    nq = seq // SB_BQ
    n_pairs = SB_WIDTH // LANES
    return pl.pallas_call(
        _sb_kernel,
        out_shape=jax.ShapeDtypeStruct(q.shape, BF16),
        grid=(batch, n_pairs, nq),
        in_specs=[pl.BlockSpec((SB_BQ, LANES), lambda b, h, i: (b * nq + i, h)),
                  pl.BlockSpec((seq, LANES), lambda b, h, i: (b, h)),
                  pl.BlockSpec((seq, LANES), lambda b, h, i: (b, h)),
                  pl.BlockSpec((SB_BK, SB_BK), lambda b, h, i: (0, 0)),
                  pl.BlockSpec((1, LANES), lambda b, h, i: (0, 0))],
        out_specs=pl.BlockSpec((SB_BQ, LANES), lambda b, h, i: (b * nq + i, h)),
        compiler_params=pltpu.CompilerParams(
            dimension_semantics=("arbitrary", "arbitrary", "arbitrary"), vmem_limit_bytes=VMEM_LIMIT),
        name="sb_attn",
    )(q, k, v, u_tri, g_row)


def _da_kernel(q_ref, k_ref, v_ref, lam_ref, g_ref, o_ref, *, lambda_init):
    qi = pl.program_id(2)
    qs = _stack_halves(q_ref[...])
    rows = 2 * DA_BQ
    t_local, s_local = _local_positions(rows, DA_BQ, DA_BK)

    def block(kb, m, l, acc, mask):
        start = pl.multiple_of(kb * DA_BK, DA_BK)
        ks = k_ref[pl.ds(start, DA_BK), :]
        vs = v_ref[pl.ds(start, DA_BK), :]
        s = _dot_nt(qs, ks)
        if mask is not None:
            s = jnp.where(mask, s, -jnp.inf)
        m_blk = jnp.max(s, axis=1, keepdims=True)
        m_new = m_blk if m is None else jnp.maximum(m, m_blk)
        p = jnp.exp(s - m_new)
        pv = _dot(p.astype(BF16), vs)
        ps = jnp.sum(p, axis=1, keepdims=True)
        if m is None:
            return m_new, ps, pv
        alpha = jnp.exp(m - m_new)
        return m_new, alpha * l + ps, alpha * acc + pv

    per_q = DA_BQ // DA_BK
    m = l = acc = None
    for j in range(per_q):
        m, l, acc = block(qi * per_q + j, m, l, acc, s_local + j * DA_BK <= t_local)
    m, l, acc = lax.fori_loop(0, qi * per_q, lambda kb, st: block(kb, st[0], st[1], st[2], None), (m, l, acc))
    o_maps = acc / l

    lp = lam_ref[...]
    lam = (jnp.exp(jnp.sum(lp[0:1] * lp[1:2], keepdims=True))
           - jnp.exp(jnp.sum(lp[2:3] * lp[3:4], keepdims=True)) + lambda_init)
    o = o_maps[:DA_BQ] - lam * o_maps[DA_BQ:]
    ms = jnp.mean(o * o, axis=1, keepdims=True)
    o_ref[...] = (o * lax.rsqrt(ms + LN_EPS) * (g_ref[...] * (1.0 - lambda_init))).astype(BF16)


def _da_attention(q, k, v, lam_p, g_row, batch, seq, lambda_init):
    nq = seq // DA_BQ
    return pl.pallas_call(
        functools.partial(_da_kernel, lambda_init=lambda_init),
        out_shape=jax.ShapeDtypeStruct(q.shape, BF16),
        grid=(batch, DA_HEADS, nq),
        in_specs=[pl.BlockSpec((DA_BQ, LANES), lambda b, h, i: (b * nq + i, h)),
                  pl.BlockSpec((seq, LANES), lambda b, h, i: (b, h)),
                  pl.BlockSpec((seq, LANES), lambda b, h, i: (b, h)),
                  pl.BlockSpec(lam_p.shape, lambda b, h, i: (0, 0)),
                  pl.BlockSpec((1, LANES), lambda b, h, i: (0, 0))],
        out_specs=pl.BlockSpec((DA_BQ, LANES), lambda b, h, i: (b * nq + i, h)),
        compiler_params=pltpu.CompilerParams(
            dimension_semantics=("arbitrary", "arbitrary", "arbitrary"), vmem_limit_bytes=VMEM_LIMIT),
        name="da_attn",
    )(q, k, v, lam_p, g_row)


def _route_kernel(sb_ref, da_ref, x_ref, wot_ref, wob_ref, g_ref, b_ref, rw_ref, rb_ref, u_ref,
                  xg_ref, cls_ref, rank_ref, cnt_ref, *, alpha):
    tm = x_ref.shape[0]

    @pl.when(pl.program_id(0) == 0)
    def _():
        cnt_ref[...] = jnp.zeros_like(cnt_ref)

    mix = _dot(sb_ref[...], wot_ref[...]) + _dot(da_ref[...], wob_ref[...])
    x1 = _layer_norm(alpha * x_ref[...] + mix, g_ref[...], b_ref[...])
    xg_ref[:, :D_MODEL] = x1

    x_hi = x1.astype(BF16)
    x_lo = (x1 - x_hi.astype(F32)).astype(BF16)
    by_hi = _dot_nt(rw_ref[...], x_hi)
    logits = (by_hi[0:ROUTER_ROWS] + by_hi[ROUTER_ROWS:]) + _dot_nt(rw_ref[0:ROUTER_ROWS], x_lo)
    scores = _sigmoid(logits)
    sel = scores + rb_ref[...]
    grow = lax.broadcasted_iota(jnp.int32, (8, tm), 0)
    a, b, c, d = sel[0:8], sel[8:16], sel[16:24], sel[24:32]
    hi1, lo1 = jnp.maximum(a, b), jnp.minimum(a, b)
    hi2, lo2 = jnp.maximum(c, d), jnp.minimum(c, d)
    top1 = jnp.maximum(hi1, hi2)
    top2 = jnp.maximum(jnp.minimum(hi1, hi2), jnp.maximum(lo1, lo2))
    gscore = jnp.where(grow < N_GROUPS, top1 + top2, -jnp.inf)
    gmax = jnp.max(gscore, axis=0, keepdims=True)
    grp = jnp.min(jnp.where(gscore == gmax, grow, 8), axis=0, keepdims=True)
    gsel = grow == grp

    one = lambda m: jnp.where(m, 1, 0)
    ranks = (one(b > a) + one(c > a) + one(d > a),
             one(a >= b) + one(c > b) + one(d > b),
             one(a >= c) + one(b >= c) + one(d > c),
             one(a >= d) + one(b >= d) + one(c >= d))
    sc = (scores[0:8], scores[8:16], scores[16:24], scores[24:32])
    chosen = [jnp.where(gsel, one(r < TOP_K), 0) for r in ranks]
    picked = [jnp.sum(ch, axis=0, keepdims=True) for ch in chosen]
    num = [jnp.sum(jnp.where(ch == 1, s, 0.0), axis=0, keepdims=True) for ch, s in zip(chosen, sc)]
    den = num[0] + num[1] + num[2] + num[3]
    gate = [n_ / den for n_ in num]
    g_first = jnp.where(picked[0] == 1, gate[0], jnp.where(picked[1] == 1, gate[1], gate[2]))
    g_second = jnp.where(picked[3] == 1, gate[3], jnp.where(picked[2] == 1, gate[2], gate[1]))
    code = picked[0] + 2 * picked[1] + 4 * picked[2] + 8 * picked[3]
    pair = jnp.where(code == 3, 0, jnp.where(code == 5, 1, jnp.where(code == 6, 2,
           jnp.where(code == 9, 3, jnp.where(code == 10, 4, 5)))))
    cls = grp * N_PAIRS + pair

    crow = lax.broadcasted_iota(jnp.int32, (CLASS_ROWS, tm), 0)
    onehot = jnp.where(crow == cls, 1.0, 0.0)
    prefix = _dot(onehot.astype(BF16), u_ref[...])
    before = cnt_ref[:, 0:1]
    rank = jnp.sum(onehot * (prefix + before), axis=0, keepdims=True) - 1.0
    cnt_ref[...] = cnt_ref[...] + jnp.sum(onehot, axis=1, keepdims=True)
    cls_ref[0] = cls
    rank_ref[0] = rank.astype(jnp.int32)

    trow = lax.broadcasted_iota(jnp.int32, (GATE_COLS, tm), 0)
    gates_t = jnp.where(trow == 0, g_first, jnp.where(trow == 1, g_second, 0.0))
    xg_ref[:, D_MODEL:] = gates_t.T


TOP_K = 2


def _proj_route(sb, da, x2d, wo_top, wo_bot, g, b, rw_t, rb_col, u_tri, alpha):
    n = x2d.shape[0]
    tm = TM_ROUTE
    nt = n // tm
    row = lambda i: (i, 0)
    const = lambda i: (0, 0)
    return pl.pallas_call(
        functools.partial(_route_kernel, alpha=alpha),
        out_shape=(jax.ShapeDtypeStruct((n, ROW_W), F32),
                   jax.ShapeDtypeStruct((nt, 1, tm), jnp.int32),
                   jax.ShapeDtypeStruct((nt, 1, tm), jnp.int32),
                   jax.ShapeDtypeStruct((CLASS_ROWS, LANES), F32)),
        grid=(nt,),
        in_specs=[pl.BlockSpec((tm, SB_WIDTH), row), pl.BlockSpec((tm, DA_WIDTH), row),
                  pl.BlockSpec((tm, D_MODEL), row),
                  pl.BlockSpec(wo_top.shape, const), pl.BlockSpec(wo_bot.shape, const),
                  pl.BlockSpec((1, D_MODEL), const), pl.BlockSpec((1, D_MODEL), const),
                  pl.BlockSpec(rw_t.shape, const), pl.BlockSpec(rb_col.shape, const),
                  pl.BlockSpec((tm, tm), const)],
        out_specs=(pl.BlockSpec((tm, ROW_W), row),
                   pl.BlockSpec((1, 1, tm), lambda i: (i, 0, 0)),
                   pl.BlockSpec((1, 1, tm), lambda i: (i, 0, 0)),
                   pl.BlockSpec((CLASS_ROWS, LANES), const)),
        compiler_params=pltpu.CompilerParams(dimension_semantics=("arbitrary",),
                                             vmem_limit_bytes=VMEM_LIMIT),
        name="proj_route",
    )(sb, da, x2d, wo_top, wo_bot, g, b, rw_t, rb_col, u_tri)


def _dispatch_kernel(pos_ref, xg_ref, zeros_ref, xs_ref, sem):
    del zeros_ref
    tm = xg_ref.shape[0]
    base = pl.program_id(0) * tm

    def issue(g, _):
        for k in range(ROW_UNROLL):
            r = g * ROW_UNROLL + k
            pltpu.make_async_copy(xg_ref.at[pl.ds(r, 1), :],
                                  xs_ref.at[pl.ds(pos_ref[base + r], 1), :], sem).start()
        return 0

    lax.fori_loop(0, tm // ROW_UNROLL, issue, 0)
    pltpu.make_async_copy(xg_ref, xs_ref.at[pl.ds(0, tm), :], sem).wait()


def _dispatch(pos, xg, n_rows):
    n = xg.shape[0]
    tm = TM_DISPATCH
    zeros = jnp.zeros((n_rows, ROW_W), F32)
    return pl.pallas_call(
        _dispatch_kernel,
        out_shape=jax.ShapeDtypeStruct((n_rows, ROW_W), F32),
        grid_spec=pltpu.PrefetchScalarGridSpec(
            num_scalar_prefetch=1,
            grid=(n // tm,),
            in_specs=[pl.BlockSpec((tm, ROW_W), lambda i, pos: (i, 0)),
                      pl.BlockSpec(memory_space=pl.ANY)],
            out_specs=pl.BlockSpec(memory_space=pl.ANY),
            scratch_shapes=[pltpu.SemaphoreType.DMA]),
        input_output_aliases={2: 0},
        compiler_params=pltpu.CompilerParams(dimension_semantics=("arbitrary",),
                                             vmem_limit_bytes=VMEM_LIMIT),
        name="dispatch",
    )(pos, xg, zeros)


def _moe_kernel(e1_ref, e2_ref, nv_ref, xs_ref, wg1, wu1, wd1, wg2, wu2, wd2, ys_ref, *, alpha):
    @pl.when(pl.program_id(0) < nv_ref[0])
    def _():
        x = xs_ref[:, :D_MODEL]
        gates = xs_ref[:, D_MODEL:]
        xb = x.astype(BF16)

        def expert(wg, wu, wd):
            gt = _dot(xb, wg[...])
            h = gt * _sigmoid(gt) * _dot(xb, wu[...])
            return _dot(h.astype(BF16), wd[...])

        ffn = gates[:, 0:1] * expert(wg1, wu1, wd1) + gates[:, 1:2] * expert(wg2, wu2, wd2)
        ys_ref[...] = alpha * x + ffn

    @pl.when(pl.program_id(0) >= nv_ref[0])
    def _():
        ys_ref[...] = jnp.zeros_like(ys_ref)


def _moe(tile_e1, tile_e2, n_valid, xs, wg, wu, wd, layer, alpha):
    n_rows = xs.shape[0]
    n_tiles = n_rows // MOE_TILE
    tile = lambda i, e1, e2, nv: (jnp.maximum(jnp.minimum(i, nv[0] - 1), 0), 0)
    first = lambda i, e1, e2, nv: (layer, e1[i], 0, 0)
    second = lambda i, e1, e2, nv: (layer, e2[i], 0, 0)
    up_spec = lambda f: pl.BlockSpec((None, None, D_MODEL, D_FF), f)
    down_spec = lambda f: pl.BlockSpec((None, None, D_FF, D_MODEL), f)
    return pl.pallas_call(
        functools.partial(_moe_kernel, alpha=alpha),
        out_shape=jax.ShapeDtypeStruct((n_rows, D_MODEL), F32),
        grid_spec=pltpu.PrefetchScalarGridSpec(
            num_scalar_prefetch=3,
            grid=(n_tiles,),
            in_specs=[pl.BlockSpec((MOE_TILE, ROW_W), tile),
                      up_spec(first), up_spec(first), down_spec(first),
                      up_spec(second), up_spec(second), down_spec(second)],
            out_specs=pl.BlockSpec((MOE_TILE, D_MODEL), lambda i, e1, e2, nv: (i, 0))),
        compiler_params=pltpu.CompilerParams(dimension_semantics=("arbitrary",),
                                             vmem_limit_bytes=VMEM_LIMIT),
        name="moe",
    )(tile_e1, tile_e2, n_valid, xs, wg, wu, wd, wg, wu, wd)


def _combine_kernel(pos_ref, ys_ref, p_ref, g_ref, b_ref, wpg_ref, bpg_ref, wple_ref, o_ref, buf, sem):
    tm = o_ref.shape[0]
    i = pl.program_id(0)
    slot = i % 2

    def gather(tile, dst_slot):
        def issue(g, _):
            for k in range(ROW_UNROLL):
                r = g * ROW_UNROLL + k
                pltpu.make_async_copy(ys_ref.at[pl.ds(pos_ref[tile * tm + r], 1), :],
                                      buf.at[dst_slot, pl.ds(r, 1), :], sem.at[dst_slot]).start()
            return 0
        lax.fori_loop(0, tm // ROW_UNROLL, issue, 0)

    @pl.when(i == 0)
    def _():
        gather(0, 0)

    @pl.when(i + 1 < pl.num_programs(0))
    def _():
        gather(i + 1, 1 - slot)

    pltpu.make_async_copy(ys_ref.at[pl.ds(0, tm), :], buf.at[slot], sem.at[slot]).wait()
    x2 = _layer_norm(buf[slot], g_ref[...], b_ref[...])
    gate = _sigmoid(_dot(x2.astype(BF16), wpg_ref[...]) + bpg_ref[...])
    o_ref[...] = x2 + gate * _dot(p_ref[...].astype(BF16), wple_ref[...])


def _combine_ple(pos, ys, p3d, g, b, wpg, bpg, wple, layer):
    n = p3d.shape[1]
    tm = TM_ROWS
    row = lambda i, pos: (i, 0)
    const = lambda i, pos: (0, 0)
    layer_const = lambda i, pos: (layer, 0, 0)
    return pl.pallas_call(
        _combine_kernel,
        out_shape=jax.ShapeDtypeStruct((n, D_MODEL), F32),
        grid_spec=pltpu.PrefetchScalarGridSpec(
            num_scalar_prefetch=1,
            grid=(n // tm,),
            in_specs=[pl.BlockSpec(memory_space=pl.ANY),
                      pl.BlockSpec((None, tm, PLE_DIM), lambda i, pos: (layer, i, 0)),
                      pl.BlockSpec((1, D_MODEL), const), pl.BlockSpec((1, D_MODEL), const),
                      pl.BlockSpec((None,) + wpg.shape[1:], layer_const),
                      pl.BlockSpec((1, D_MODEL), const),
                      pl.BlockSpec((None,) + wple.shape[1:], layer_const)],
            out_specs=pl.BlockSpec((tm, D_MODEL), row),
            scratch_shapes=[pltpu.VMEM((2, tm, D_MODEL), F32), pltpu.SemaphoreType.DMA((2,))]),
        compiler_params=pltpu.CompilerParams(dimension_semantics=("arbitrary",),
                                             vmem_limit_bytes=VMEM_LIMIT),
        name="combine_ple",
    )(pos, ys, p3d, g, b, wpg, bpg, wple)


def _tri(n, inclusive_lower):
    j = np.arange(n)[:, None]
    s = np.arange(n)[None, :]
    return jnp.asarray((j >= s) if inclusive_lower else (j <= s), dtype=BF16)


def kernel(x, p, positions, w_in, w_o, sb_norm_g, da_lambda, da_subln_g, ln1_g, ln1_b, ln2_g, ln2_b,
           router_w, router_b, w_gate, w_up, w_down, w_ple, w_ple_gate, b_ple_gate):
    batch, seq, d = x.shape
    depth = w_in.shape[0]
    n = batch * seq
    alpha = (2 * depth) ** 0.25
    max_tiles = n // MOE_TILE + N_CLASSES
    n_rows = max_tiles * MOE_TILE

    inv_freq = ROPE_THETA ** (-jnp.arange(0, ROPE_DIM, 2, dtype=F32) / ROPE_DIM)
    invf_row = jnp.tile(jnp.concatenate([inv_freq, inv_freq, jnp.zeros((HEAD_DIM - ROPE_DIM,), F32)]),
                        LANES // HEAD_DIM)[None, :]
    cos_t, sin_t = _rope_tables(positions.reshape(n, 1), invf_row)

    src = np.array([[4 * g + l for g in range(N_GROUPS)] for l in range(EXPERTS_PER_GROUP)])
    rw_t = jnp.zeros((EXPERTS_PER_GROUP, 8, d), F32).at[:, :N_GROUPS, :].set(router_w.T[src])
    rw_t = rw_t.reshape(ROUTER_ROWS, d)
    rw_hi = rw_t.astype(BF16)
    rw_pad = jnp.concatenate([rw_hi, (rw_t - rw_hi.astype(F32)).astype(BF16)], axis=0)
    rb_col = jnp.zeros((EXPERTS_PER_GROUP, 8), F32).at[:, :N_GROUPS].set(router_b[src]).reshape(ROUTER_ROWS, 1)

    u_sb = _tri(SB_BK, True)
    u_rank = _tri(TM_ROUTE, False)
    class_e1 = jnp.asarray(_CLASS_E1)
    class_e2 = jnp.asarray(_CLASS_E2)

    w_in_b, wg_b, wu_b, wd_b = (w.astype(BF16) for w in (w_in, w_gate, w_up, w_down))
    wpg_b, wple_b = w_ple_gate.astype(BF16), w_ple.astype(BF16)
    p3d = p.reshape(depth, n, PLE_DIM)

    x2d = x.reshape(n, d)
    for i in range(depth):
        lambda_init = 0.8 - 0.6 * math.exp(-0.3 * i)
        sbq, sbk, sbv, daq, dak, dav = _qkv(x2d, w_in_b, cos_t, sin_t, i)
        g_sb = jnp.tile(sb_norm_g[i], LANES // HEAD_DIM)[None, :]
        sb = _sb_attention(sbq, sbk, sbv, u_sb, g_sb, batch, seq)
        da = _da_attention(daq, dak, dav, da_lambda[i], da_subln_g[i][None, :], batch, seq, lambda_init)

        wo = w_o[i].astype(BF16)
        xg, cls, rank, counts = _proj_route(sb, da, x2d, wo[:SB_WIDTH], wo[SB_WIDTH:],
                                            ln1_g[i][None, :], ln1_b[i][None, :], rw_pad, rb_col, u_rank, alpha)

        cnt = counts[:N_CLASSES, 0].astype(jnp.int32)
        tiles_c = (cnt + MOE_TILE - 1) // MOE_TILE
        tile_end = jnp.cumsum(tiles_c)
        row_off = (tile_end - tiles_c) * MOE_TILE
        pos = row_off[cls.reshape(n)] + rank.reshape(n)
        n_valid = tile_end[-1:]
        tile_ids = jnp.minimum(jnp.arange(max_tiles, dtype=jnp.int32), n_valid[0] - 1)
        tile_cls = jnp.minimum(jnp.sum((tile_ids[:, None] >= tile_end[None, :]).astype(jnp.int32), axis=1),
                               N_CLASSES - 1)

        xs = _dispatch(pos, xg, n_rows)
        ys = _moe(class_e1[tile_cls], class_e2[tile_cls], n_valid.astype(jnp.int32), xs,
                  wg_b, wu_b, wd_b, i, alpha)
        x2d = _combine_ple(pos, ys, p3d, ln2_g[i][None, :], ln2_b[i][None, :],
                           wpg_b, b_ple_gate[i][None, :], wple_b, i)
    return x2d.reshape(batch, seq, d)
```

```python
import functools
import math

import numpy as np
import jax
import jax.numpy as jnp
from jax import lax
from jax.experimental import pallas as pl
from jax.experimental.pallas import tpu as pltpu

D_MODEL = 1024
HEAD_DIM = 64
SB_WIDTH = 512
DA_WIDTH = 512
DA_HEADS = 4
ROPE_DIM = 16
ROPE_THETA = 500000.0
N_EXPERTS = 16
N_GROUPS = 4
EXPERTS_PER_GROUP = 4
D_FF = 512
PLE_DIM = 256
LN_EPS = 1e-5

LANES = 128
N_PAIRS = 6
N_CLASSES = N_GROUPS * N_PAIRS
CLASS_ROWS = 32
ROUTER_ROWS = 4 * 8
ROW_UNROLL = 8
GATE_COLS = LANES
ROW_W = D_MODEL + GATE_COLS

MOE_TILE = 256
SB_BQ = 512
SB_BK = 256
DA_BQ = 1024
DA_BK = 512
SB_UNDERFLOW = 104.0
LOG2E = 1.4426950408889634
TM_QKV = 512
TM_ROUTE = 512
TM_DISPATCH = 2048
TM_ROWS = 512
VMEM_LIMIT = 56 * 1024 * 1024

F32 = jnp.float32
BF16 = jnp.bfloat16

_PAIR_LOCAL = [(0, 1), (0, 2), (1, 2), (0, 3), (1, 3), (2, 3)]
_CLASS_E1 = np.array([g * 4 + _PAIR_LOCAL[p][0] for g in range(N_GROUPS) for p in range(N_PAIRS)], np.int32)
_CLASS_E2 = np.array([g * 4 + _PAIR_LOCAL[p][1] for g in range(N_GROUPS) for p in range(N_PAIRS)], np.int32)


def _dot(a, b):
    return jnp.dot(a, b, preferred_element_type=F32)


def _dot_nt(a, b, precision=None):
    return lax.dot_general(a, b, (((1,), (1,)), ((), ())), precision=precision,
                           preferred_element_type=F32)


def _sigmoid(x):
    return 1.0 / (1.0 + jnp.exp(-x))


def _layer_norm(y, g, b):
    mu = jnp.mean(y, axis=-1, keepdims=True)
    yc = y - mu
    var = jnp.mean(yc * yc, axis=-1, keepdims=True)
    return yc * lax.rsqrt(var + LN_EPS) * g + b


def _rope_table_kernel(pos_ref, invf_ref, cos_ref, sin_ref):
    ang = pos_ref[...].astype(F32) * invf_ref[...]
    cos_ref[...] = jnp.cos(ang)
    sin_ref[...] = jnp.sin(ang)


def _rope_tables(pos_col, invf_row):
    n = pos_col.shape[0]
    tm = 1024
    return pl.pallas_call(
        _rope_table_kernel,
        out_shape=(jax.ShapeDtypeStruct((n, LANES), F32), jax.ShapeDtypeStruct((n, LANES), F32)),
        grid=(n // tm,),
        in_specs=[pl.BlockSpec((tm, 1), lambda i: (i, 0)),
                  pl.BlockSpec((1, LANES), lambda i: (0, 0))],
        out_specs=(pl.BlockSpec((tm, LANES), lambda i: (i, 0)),
                   pl.BlockSpec((tm, LANES), lambda i: (i, 0))),
        compiler_params=pltpu.CompilerParams(dimension_semantics=("arbitrary",)),
        name="rope_tables",
    )(pos_col, invf_row)


def _qkv_kernel(x_ref, w_ref, cos_ref, sin_ref, sbq, sbk, sbv, daq, dak, dav):
    xb = x_ref[...].astype(BF16)
    lane = lax.broadcasted_iota(jnp.int32, (1, LANES), 1) % HEAD_DIM
    half = ROPE_DIM // 2
    cosr = jnp.where(lane < ROPE_DIM, cos_ref[...], 1.0)
    sin_lo = jnp.where(lane < half, -sin_ref[...], 0.0)
    sin_hi = jnp.where((lane >= half) & (lane < ROPE_DIM), sin_ref[...], 0.0)
    scale = HEAD_DIM ** -0.5
    outs = (sbq, sbk, sbv, daq, dak, dav)
    for c, o_ref in enumerate(outs):
        hc = _dot(xb, w_ref[:, c * SB_WIDTH:(c + 1) * SB_WIDTH])
        for blk in range(SB_WIDTH // LANES):
            h = hc[:, blk * LANES:(blk + 1) * LANES]
            if c in (3, 4):
                h = (h * cosr + pltpu.roll(h, LANES - half, 1) * sin_lo
                     + pltpu.roll(h, half, 1) * sin_hi)
            if c in (0, 3):
                h = h * scale
            o_ref[:, blk * LANES:(blk + 1) * LANES] = h.astype(BF16)


def _qkv(x2d, w_bf16, cos_t, sin_t, layer):
    n = x2d.shape[0]
    tm = TM_QKV
    row = lambda i: (i, 0)
    out_sd = jax.ShapeDtypeStruct((n, SB_WIDTH), BF16)
    return pl.pallas_call(
        _qkv_kernel,
        out_shape=(out_sd,) * 6,
        grid=(n // tm,),
        in_specs=[pl.BlockSpec((tm, D_MODEL), row),
                  pl.BlockSpec((None,) + w_bf16.shape[1:], lambda i: (layer, 0, 0)),
                  pl.BlockSpec((tm, LANES), row),
                  pl.BlockSpec((tm, LANES), row)],
        out_specs=(pl.BlockSpec((tm, SB_WIDTH), row),) * 6,
        compiler_params=pltpu.CompilerParams(dimension_semantics=("arbitrary",),
                                             vmem_limit_bytes=VMEM_LIMIT),
        name="qkv",
    )(x2d, w_bf16, cos_t, sin_t)


def _stack_halves(q):
    first = lax.broadcasted_iota(jnp.int32, (1, LANES), 1) < HEAD_DIM
    zero = jnp.zeros_like(q)
    return jnp.concatenate([jnp.where(first, q, zero), jnp.where(first, zero, q)], axis=0)


def _local_positions(rows, bq, bk):
    r = lax.broadcasted_iota(jnp.int32, (rows, bk), 0)
    return jnp.where(r >= bq, r - bq, r), lax.broadcasted_iota(jnp.int32, (rows, bk), 1)


def _sb_kernel(q_ref, k_ref, v_ref, u_ref, g_ref, o_ref):
    qi = pl.program_id(2)
    qs = _stack_halves(q_ref[...])
    rows = 2 * SB_BQ
    t_local, s_local = _local_positions(rows, SB_BQ, SB_BK)
    u = u_ref[...]

    def block(qm, kb, carry, acc, mask):
        start = pl.multiple_of(kb * SB_BK, SB_BK)
        ks = k_ref[pl.ds(start, SB_BK), :]
        vs = v_ref[pl.ds(start, SB_BK), :]
        z = _dot_nt(qm, ks)
        sp = jnp.maximum(z, 0.0) + jnp.log(1.0 + jnp.exp2(jnp.abs(z) * (-LOG2E)))
        if mask is not None:
            sp = jnp.where(mask, sp, 0.0)
        sp_hi = sp.astype(BF16)
        sp_lo = (sp - sp_hi.astype(F32)).astype(BF16)
        tail = _dot(sp_hi, u) + _dot(sp_lo, u)
        a = jnp.exp(z - tail - carry)
        if mask is not None:
            a = jnp.where(mask, a, 0.0)
        acc = acc + _dot(a.astype(BF16), vs)
        carry = carry + jnp.sum(sp, axis=1, keepdims=True)
        return carry, acc

    carry, acc = jnp.zeros((rows, 1), F32), jnp.zeros((rows, LANES), F32)
    per_q = SB_BQ // SB_BK
    for j in reversed(range(per_q)):
        carry, acc = block(qs, qi * per_q + j, carry, acc, s_local + j * SB_BK < t_local)

    def more(st):
        return (st[0] >= 0) & (jnp.min(st[1]) < SB_UNDERFLOW)

    def step(st):
        carry, acc = block(qs, st[0], st[1], st[2], None)
        return st[0] - 1, carry, acc

    _, carry, acc = lax.while_loop(more, step, (qi * per_q - 1, carry, acc))

    first = lax.broadcasted_iota(jnp.int32, (1, LANES), 1) < HEAD_DIM
    o = jnp.where(first, acc[:SB_BQ], acc[SB_BQ:])
    sq = o * o
    ss0 = jnp.sum(jnp.where(first, sq, 0.0), axis=1, keepdims=True)
    ss1 = jnp.sum(jnp.where(first, 0.0, sq), axis=1, keepdims=True)
    ms = jnp.where(first, ss0, ss1) * (1.0 / HEAD_DIM)
    o_ref[...] = (o * lax.rsqrt(ms + LN_EPS) * g_ref[...]).astype(BF16)


def _sb_attention(q, k, v, u_tri, g_row, batch, seq):
    nq = seq // SB_BQ
    n_pairs = SB_WIDTH // LANES
    return pl.pallas_call(
        _sb_kernel,
        out_shape=jax.ShapeDtypeStruct(q.shape, BF16),
        grid=(batch, n_pairs, nq),
        in_specs=[pl.BlockSpec((SB_BQ, LANES), lambda b, h, i: (b * nq + i, h)),
                  pl.BlockSpec((seq, LANES), lambda b, h, i: (b, h)),
                  pl.BlockSpec((seq, LANES), lambda b, h, i: (b, h)),
                  pl.BlockSpec((SB_BK, SB_BK), lambda b, h, i: (0, 0)),
                  pl.BlockSpec((1, LANES), lambda b, h, i: (0, 0))],
        out_specs=pl.BlockSpec((SB_BQ, LANES), lambda b, h, i: (b * nq + i, h)),
        compiler_params=pltpu.CompilerParams(
            dimension_semantics=("arbitrary", "arbitrary", "arbitrary"), vmem_limit_bytes=VMEM_LIMIT),
        name="sb_attn",
    )(q, k, v, u_tri, g_row)


def _da_kernel(q_ref, k_ref, v_ref, lam_ref, g_ref, o_ref, *, lambda_init):
    qi = pl.program_id(2)
    qs = _stack_halves(q_ref[...])
    rows = 2 * DA_BQ
    t_local, s_local = _local_positions(rows, DA_BQ, DA_BK)

    def block(kb, m, l, acc, mask):
        start = pl.multiple_of(kb * DA_BK, DA_BK)
        ks = k_ref[pl.ds(start, DA_BK), :]
        vs = v_ref[pl.ds(start, DA_BK), :]
        s = _dot_nt(qs, ks)
        if mask is not None:
            s = jnp.where(mask, s, -jnp.inf)
        m_blk = jnp.max(s, axis=1, keepdims=True)
        m_new = m_blk if m is None else jnp.maximum(m, m_blk)
        p = jnp.exp(s - m_new)
        pv = _dot(p.astype(BF16), vs)
        ps = jnp.sum(p, axis=1, keepdims=True)
        if m is None:
            return m_new, ps, pv
        alpha = jnp.exp(m - m_new)
        return m_new, alpha * l + ps, alpha * acc + pv

    per_q = DA_BQ // DA_BK
    m = l = acc = None
    for j in range(per_q):
        m, l, acc = block(qi * per_q + j, m, l, acc, s_local + j * DA_BK <= t_local)
    m, l, acc = lax.fori_loop(0, qi * per_q, lambda kb, st: block(kb, st[0], st[1], st[2], None), (m, l, acc))
    o_maps = acc / l

    lp = lam_ref[...]
    lam = (jnp.exp(jnp.sum(lp[0:1] * lp[1:2], keepdims=True))
           - jnp.exp(jnp.sum(lp[2:3] * lp[3:4], keepdims=True)) + lambda_init)
    o = o_maps[:DA_BQ] - lam * o_maps[DA_BQ:]
    ms = jnp.mean(o * o, axis=1, keepdims=True)
    o_ref[...] = (o * lax.rsqrt(ms + LN_EPS) * (g_ref[...] * (1.0 - lambda_init))).astype(BF16)


def _da_attention(q, k, v, lam_p, g_row, batch, seq, lambda_init):
    nq = seq // DA_BQ
    return pl.pallas_call(
        functools.partial(_da_kernel, lambda_init=lambda_init),
        out_shape=jax.ShapeDtypeStruct(q.shape, BF16),
        grid=(batch, DA_HEADS, nq),
        in_specs=[pl.BlockSpec((DA_BQ, LANES), lambda b, h, i: (b * nq + i, h)),
                  pl.BlockSpec((seq, LANES), lambda b, h, i: (b, h)),
                  pl.BlockSpec((seq, LANES), lambda b, h, i: (b, h)),
                  pl.BlockSpec(lam_p.shape, lambda b, h, i: (0, 0)),
                  pl.BlockSpec((1, LANES), lambda b, h, i: (0, 0))],
        out_specs=pl.BlockSpec((DA_BQ, LANES), lambda b, h, i: (b * nq + i, h)),
        compiler_params=pltpu.CompilerParams(
            dimension_semantics=("arbitrary", "arbitrary", "arbitrary"), vmem_limit_bytes=VMEM_LIMIT),
        name="da_attn",
    )(q, k, v, lam_p, g_row)


def _route_kernel(sb_ref, da_ref, x_ref, wot_ref, wob_ref, g_ref, b_ref, rw_ref, rb_ref, u_ref,
                  xg_ref, cls_ref, rank_ref, cnt_ref, *, alpha):
    tm = x_ref.shape[0]

    @pl.when(pl.program_id(0) == 0)
    def _():
        cnt_ref[...] = jnp.zeros_like(cnt_ref)

    mix = _dot(sb_ref[...], wot_ref[...]) + _dot(da_ref[...], wob_ref[...])
    x1 = _layer_norm(alpha * x_ref[...] + mix, g_ref[...], b_ref[...])
    xg_ref[:, :D_MODEL] = x1

    x_hi = x1.astype(BF16)
    x_lo = (x1 - x_hi.astype(F32)).astype(BF16)
    by_hi = _dot_nt(rw_ref[...], x_hi)
    logits = (by_hi[0:ROUTER_ROWS] + by_hi[ROUTER_ROWS:]) + _dot_nt(rw_ref[0:ROUTER_ROWS], x_lo)
    scores = _sigmoid(logits)
    sel = scores + rb_ref[...]
    grow = lax.broadcasted_iota(jnp.int32, (8, tm), 0)
    a, b, c, d = sel[0:8], sel[8:16], sel[16:24], sel[24:32]
    hi1, lo1 = jnp.maximum(a, b), jnp.minimum(a, b)
    hi2, lo2 = jnp.maximum(c, d), jnp.minimum(c, d)
    top1 = jnp.maximum(hi1, hi2)
    top2 = jnp.maximum(jnp.minimum(hi1, hi2), jnp.maximum(lo1, lo2))
    gscore = jnp.where(grow < N_GROUPS, top1 + top2, -jnp.inf)
    gmax = jnp.max(gscore, axis=0, keepdims=True)
    grp = jnp.min(jnp.where(gscore == gmax, grow, 8), axis=0, keepdims=True)
    gsel = grow == grp

    one = lambda m: jnp.where(m, 1, 0)
    ranks = (one(b > a) + one(c > a) + one(d > a),
             one(a >= b) + one(c > b) + one(d > b),
             one(a >= c) + one(b >= c) + one(d > c),
             one(a >= d) + one(b >= d) + one(c >= d))
    sc = (scores[0:8], scores[8:16], scores[16:24], scores[24:32])
    chosen = [jnp.where(gsel, one(r < TOP_K), 0) for r in ranks]
    picked = [jnp.sum(ch, axis=0, keepdims=True) for ch in chosen]
    num = [jnp.sum(jnp.where(ch == 1, s, 0.0), axis=0, keepdims=True) for ch, s in zip(chosen, sc)]
    den = num[0] + num[1] + num[2] + num[3]
    gate = [n_ / den for n_ in num]
    g_first = jnp.where(picked[0] == 1, gate[0], jnp.where(picked[1] == 1, gate[1], gate[2]))
    g_second = jnp.where(picked[3] == 1, gate[3], jnp.where(picked[2] == 1, gate[2], gate[1]))
    code = picked[0] + 2 * picked[1] + 4 * picked[2] + 8 * picked[3]
    pair = jnp.where(code == 3, 0, jnp.where(code == 5, 1, jnp.where(code == 6, 2,
           jnp.where(code == 9, 3, jnp.where(code == 10, 4, 5)))))
    cls = grp * N_PAIRS + pair

    crow = lax.broadcasted_iota(jnp.int32, (CLASS_ROWS, tm), 0)
    onehot = jnp.where(crow == cls, 1.0, 0.0)
    prefix = _dot(onehot.astype(BF16), u_ref[...])
    before = cnt_ref[:, 0:1]
    rank = jnp.sum(onehot * (prefix + before), axis=0, keepdims=True) - 1.0
    cnt_ref[...] = cnt_ref[...] + jnp.sum(onehot, axis=1, keepdims=True)
    cls_ref[0] = cls
    rank_ref[0] = rank.astype(jnp.int32)

    trow = lax.broadcasted_iota(jnp.int32, (GATE_COLS, tm), 0)
    gates_t = jnp.where(trow == 0, g_first, jnp.where(trow == 1, g_second, 0.0))
    xg_ref[:, D_MODEL:] = gates_t.T


TOP_K = 2


def _proj_route(sb, da, x2d, wo_top, wo_bot, g, b, rw_t, rb_col, u_tri, alpha):
    n = x2d.shape[0]
    tm = TM_ROUTE
    nt = n // tm
    row = lambda i: (i, 0)
    const = lambda i: (0, 0)
    return pl.pallas_call(
        functools.partial(_route_kernel, alpha=alpha),
        out_shape=(jax.ShapeDtypeStruct((n, ROW_W), F32),
                   jax.ShapeDtypeStruct((nt, 1, tm), jnp.int32),
                   jax.ShapeDtypeStruct((nt, 1, tm), jnp.int32),
                   jax.ShapeDtypeStruct((CLASS_ROWS, LANES), F32)),
        grid=(nt,),
        in_specs=[pl.BlockSpec((tm, SB_WIDTH), row), pl.BlockSpec((tm, DA_WIDTH), row),
                  pl.BlockSpec((tm, D_MODEL), row),
                  pl.BlockSpec(wo_top.shape, const), pl.BlockSpec(wo_bot.shape, const),
                  pl.BlockSpec((1, D_MODEL), const), pl.BlockSpec((1, D_MODEL), const),
                  pl.BlockSpec(rw_t.shape, const), pl.BlockSpec(rb_col.shape, const),
                  pl.BlockSpec((tm, tm), const)],
        out_specs=(pl.BlockSpec((tm, ROW_W), row),
                   pl.BlockSpec((1, 1, tm), lambda i: (i, 0, 0)),
                   pl.BlockSpec((1, 1, tm), lambda i: (i, 0, 0)),
                   pl.BlockSpec((CLASS_ROWS, LANES), const)),
        compiler_params=pltpu.CompilerParams(dimension_semantics=("arbitrary",),
                                             vmem_limit_bytes=VMEM_LIMIT),
        name="proj_route",
    )(sb, da, x2d, wo_top, wo_bot, g, b, rw_t, rb_col, u_tri)


def _dispatch_kernel(pos_ref, xg_ref, zeros_ref, xs_ref, sem):
    del zeros_ref
    tm = xg_ref.shape[0]
    base = pl.program_id(0) * tm

    def issue(g, _):
        for k in range(ROW_UNROLL):
            r = g * ROW_UNROLL + k
            pltpu.make_async_copy(xg_ref.at[pl.ds(r, 1), :],
                                  xs_ref.at[pl.ds(pos_ref[base + r], 1), :], sem).start()
        return 0

    lax.fori_loop(0, tm // ROW_UNROLL, issue, 0)
    pltpu.make_async_copy(xg_ref, xs_ref.at[pl.ds(0, tm), :], sem).wait()


def _dispatch(pos, xg, n_rows):
    n = xg.shape[0]
    tm = TM_DISPATCH
    zeros = jnp.zeros((n_rows, ROW_W), F32)
    return pl.pallas_call(
        _dispatch_kernel,
        out_shape=jax.ShapeDtypeStruct((n_rows, ROW_W), F32),
        grid_spec=pltpu.PrefetchScalarGridSpec(
            num_scalar_prefetch=1,
            grid=(n // tm,),
            in_specs=[pl.BlockSpec((tm, ROW_W), lambda i, pos: (i, 0)),
                      pl.BlockSpec(memory_space=pl.ANY)],
            out_specs=pl.BlockSpec(memory_space=pl.ANY),
            scratch_shapes=[pltpu.SemaphoreType.DMA]),
        input_output_aliases={2: 0},
        compiler_params=pltpu.CompilerParams(dimension_semantics=("arbitrary",),
                                             vmem_limit_bytes=VMEM_LIMIT),
        name="dispatch",
    )(pos, xg, zeros)


def _moe_kernel(e1_ref, e2_ref, nv_ref, xs_ref, wg1, wu1, wd1, wg2, wu2, wd2, ys_ref, *, alpha):
    @pl.when(pl.program_id(0) < nv_ref[0])
    def _():
        x = xs_ref[:, :D_MODEL]
        gates = xs_ref[:, D_MODEL:]
        xb = x.astype(BF16)

        def expert(wg, wu, wd):
            gt = _dot(xb, wg[...])
            h = gt * _sigmoid(gt) * _dot(xb, wu[...])
            return _dot(h.astype(BF16), wd[...])

        ffn = gates[:, 0:1] * expert(wg1, wu1, wd1) + gates[:, 1:2] * expert(wg2, wu2, wd2)
        ys_ref[...] = alpha * x + ffn

    @pl.when(pl.program_id(0) >= nv_ref[0])
    def _():
        ys_ref[...] = jnp.zeros_like(ys_ref)


def _moe(tile_e1, tile_e2, n_valid, xs, wg, wu, wd, layer, alpha):
    n_rows = xs.shape[0]
    n_tiles = n_rows // MOE_TILE
    tile = lambda i, e1, e2, nv: (jnp.maximum(jnp.minimum(i, nv[0] - 1), 0), 0)
    first = lambda i, e1, e2, nv: (layer, e1[i], 0, 0)
    second = lambda i, e1, e2, nv: (layer, e2[i], 0, 0)
    up_spec = lambda f: pl.BlockSpec((None, None, D_MODEL, D_FF), f)
    down_spec = lambda f: pl.BlockSpec((None, None, D_FF, D_MODEL), f)
    return pl.pallas_call(
        functools.partial(_moe_kernel, alpha=alpha),
        out_shape=jax.ShapeDtypeStruct((n_rows, D_MODEL), F32),
        grid_spec=pltpu.PrefetchScalarGridSpec(
            num_scalar_prefetch=3,
            grid=(n_tiles,),
            in_specs=[pl.BlockSpec((MOE_TILE, ROW_W), tile),
                      up_spec(first), up_spec(first), down_spec(first),
                      up_spec(second), up_spec(second), down_spec(second)],
            out_specs=pl.BlockSpec((MOE_TILE, D_MODEL), lambda i, e1, e2, nv: (i, 0))),
        compiler_params=pltpu.CompilerParams(dimension_semantics=("arbitrary",),
                                             vmem_limit_bytes=VMEM_LIMIT),
        name="moe",
    )(tile_e1, tile_e2, n_valid, xs, wg, wu, wd, wg, wu, wd)


def _combine_kernel(pos_ref, ys_ref, p_ref, g_ref, b_ref, wpg_ref, bpg_ref, wple_ref, o_ref, buf, sem):
    tm = o_ref.shape[0]
    i = pl.program_id(0)
    slot = i % 2

    last = pl.num_programs(0) - 1

    def row_copy(tile, r, dst_slot):
        return pltpu.make_async_copy(ys_ref.at[pl.ds(pos_ref[tile * tm + r], 1), :],
                                     buf.at[dst_slot, pl.ds(r, 1), :], sem.at[dst_slot])

    def drain(s):
        pltpu.make_async_copy(ys_ref.at[pl.ds(0, tm), :], buf.at[s], sem.at[s]).wait()

    @pl.when(i == 0)
    def _():
        def issue(g, _):
            for k in range(ROW_UNROLL):
                row_copy(0, g * ROW_UNROLL + k, 0).start()
            return 0
        lax.fori_loop(0, tm // ROW_UNROLL, issue, 0)

    drain(slot)
    xin = buf[slot]
    nxt = jnp.minimum(i + 1, last)
    for r in range(tm):
        row_copy(nxt, r, 1 - slot).start()
    x2 = _layer_norm(xin, g_ref[...], b_ref[...])
    gate = _sigmoid(_dot(x2.astype(BF16), wpg_ref[...]) + bpg_ref[...])
    o_ref[...] = x2 + gate * _dot(p_ref[...].astype(BF16), wple_ref[...])

    @pl.when(i == last)
    def _():
        drain(1 - slot)


def _combine_ple(pos, ys, p3d, g, b, wpg, bpg, wple, layer):
    n = p3d.shape[1]
    tm = TM_ROWS
    row = lambda i, pos: (i, 0)
    const = lambda i, pos: (0, 0)
    layer_const = lambda i, pos: (layer, 0, 0)
    return pl.pallas_call(
        _combine_kernel,
        out_shape=jax.ShapeDtypeStruct((n, D_MODEL), F32),
        grid_spec=pltpu.PrefetchScalarGridSpec(
            num_scalar_prefetch=1,
            grid=(n // tm,),
            in_specs=[pl.BlockSpec(memory_space=pl.ANY),
                      pl.BlockSpec((None, tm, PLE_DIM), lambda i, pos: (layer, i, 0)),
                      pl.BlockSpec((1, D_MODEL), const), pl.BlockSpec((1, D_MODEL), const),
                      pl.BlockSpec((None,) + wpg.shape[1:], layer_const),
                      pl.BlockSpec((1, D_MODEL), const),
                      pl.BlockSpec((None,) + wple.shape[1:], layer_const)],
            out_specs=pl.BlockSpec((tm, D_MODEL), row),
            scratch_shapes=[pltpu.VMEM((2, tm, D_MODEL), F32), pltpu.SemaphoreType.DMA((2,))]),
        compiler_params=pltpu.CompilerParams(dimension_semantics=("arbitrary",),
                                             vmem_limit_bytes=VMEM_LIMIT),
        name="combine_ple",
    )(pos, ys, p3d, g, b, wpg, bpg, wple)


def _tri(n, inclusive_lower):
    j = np.arange(n)[:, None]
    s = np.arange(n)[None, :]
    return jnp.asarray((j >= s) if inclusive_lower else (j <= s), dtype=BF16)


def kernel(x, p, positions, w_in, w_o, sb_norm_g, da_lambda, da_subln_g, ln1_g, ln1_b, ln2_g, ln2_b,
           router_w, router_b, w_gate, w_up, w_down, w_ple, w_ple_gate, b_ple_gate):
    batch, seq, d = x.shape
    depth = w_in.shape[0]
    n = batch * seq
    alpha = (2 * depth) ** 0.25
    max_tiles = n // MOE_TILE + N_CLASSES
    n_rows = max_tiles * MOE_TILE

    inv_freq = ROPE_THETA ** (-jnp.arange(0, ROPE_DIM, 2, dtype=F32) / ROPE_DIM)
    invf_row = jnp.tile(jnp.concatenate([inv_freq, inv_freq, jnp.zeros((HEAD_DIM - ROPE_DIM,), F32)]),
                        LANES // HEAD_DIM)[None, :]
    cos_t, sin_t = _rope_tables(positions.reshape(n, 1), invf_row)

    src = np.array([[4 * g + l for g in range(N_GROUPS)] for l in range(EXPERTS_PER_GROUP)])
    rw_t = jnp.zeros((EXPERTS_PER_GROUP, 8, d), F32).at[:, :N_GROUPS, :].set(router_w.T[src])
    rw_t = rw_t.reshape(ROUTER_ROWS, d)
    rw_hi = rw_t.astype(BF16)
    rw_pad = jnp.concatenate([rw_hi, (rw_t - rw_hi.astype(F32)).astype(BF16)], axis=0)
    rb_col = jnp.zeros((EXPERTS_PER_GROUP, 8), F32).at[:, :N_GROUPS].set(router_b[src]).reshape(ROUTER_ROWS, 1)

    u_sb = _tri(SB_BK, True)
    u_rank = _tri(TM_ROUTE, False)
    class_e1 = jnp.asarray(_CLASS_E1)
    class_e2 = jnp.asarray(_CLASS_E2)

    w_in_b, wg_b, wu_b, wd_b = (w.astype(BF16) for w in (w_in, w_gate, w_up, w_down))
    wpg_b, wple_b = w_ple_gate.astype(BF16), w_ple.astype(BF16)
    p3d = p.reshape(depth, n, PLE_DIM)

    x2d = x.reshape(n, d)
    for i in range(depth):
        lambda_init = 0.8 - 0.6 * math.exp(-0.3 * i)
        sbq, sbk, sbv, daq, dak, dav = _qkv(x2d, w_in_b, cos_t, sin_t, i)
        g_sb = jnp.tile(sb_norm_g[i], LANES // HEAD_DIM)[None, :]
        sb = _sb_attention(sbq, sbk, sbv, u_sb, g_sb, batch, seq)
        da = _da_attention(daq, dak, dav, da_lambda[i], da_subln_g[i][None, :], batch, seq, lambda_init)

        wo = w_o[i].astype(BF16)
        xg, cls, rank, counts = _proj_route(sb, da, x2d, wo[:SB_WIDTH], wo[SB_WIDTH:],
                                            ln1_g[i][None, :], ln1_b[i][None, :], rw_pad, rb_col, u_rank, alpha)

        cnt = counts[:N_CLASSES, 0].astype(jnp.int32)
        tiles_c = (cnt + MOE_TILE - 1) // MOE_TILE
        tile_end = jnp.cumsum(tiles_c)
        row_off = (tile_end - tiles_c) * MOE_TILE
        pos = row_off[cls.reshape(n)] + rank.reshape(n)
        n_valid = tile_end[-1:]
        tile_ids = jnp.minimum(jnp.arange(max_tiles, dtype=jnp.int32), n_valid[0] - 1)
        tile_cls = jnp.minimum(jnp.sum((tile_ids[:, None] >= tile_end[None, :]).astype(jnp.int32), axis=1),
                               N_CLASSES - 1)

        xs = _dispatch(pos, xg, n_rows)
        ys = _moe(class_e1[tile_cls], class_e2[tile_cls], n_valid.astype(jnp.int32), xs,
                  wg_b, wu_b, wd_b, i, alpha)
        x2d = _combine_ple(pos, ys, p3d, ln2_g[i][None, :], ln2_b[i][None, :],
                           wpg_b, b_ple_gate[i][None, :], wple_b, i)
    return x2d.reshape(batch, seq, d)
```

```python
import functools
import math

import numpy as np
import jax
import jax.numpy as jnp
from jax import lax
from jax.experimental import pallas as pl
from jax.experimental.pallas import tpu as pltpu

D_MODEL = 1024
HEAD_DIM = 64
SB_WIDTH = 512
DA_WIDTH = 512
DA_HEADS = 4
ROPE_DIM = 16
ROPE_THETA = 500000.0
N_EXPERTS = 16
N_GROUPS = 4
EXPERTS_PER_GROUP = 4
D_FF = 512
PLE_DIM = 256
LN_EPS = 1e-5

LANES = 128
N_PAIRS = 6
N_CLASSES = N_GROUPS * N_PAIRS
CLASS_ROWS = 32
ROUTER_ROWS = 4 * 8
ROW_UNROLL = 8
GATE_COLS = LANES
ROW_W = D_MODEL + GATE_COLS

MOE_TILE = 256
SB_BQ = 512
SB_BK = 256
DA_BQ = 1024
DA_BK = 512
SB_UNDERFLOW = 104.0
LOG2E = 1.4426950408889634
TM_QKV = 512
TM_ROUTE = 512
TM_DISPATCH = 2048
TM_ROWS = 512
VMEM_LIMIT = 56 * 1024 * 1024

F32 = jnp.float32
BF16 = jnp.bfloat16

_PAIR_LOCAL = [(0, 1), (0, 2), (1, 2), (0, 3), (1, 3), (2, 3)]
_CLASS_E1 = np.array([g * 4 + _PAIR_LOCAL[p][0] for g in range(N_GROUPS) for p in range(N_PAIRS)], np.int32)
_CLASS_E2 = np.array([g * 4 + _PAIR_LOCAL[p][1] for g in range(N_GROUPS) for p in range(N_PAIRS)], np.int32)


def _dot(a, b):
    return jnp.dot(a, b, preferred_element_type=F32)


def _dot_nt(a, b, precision=None):
    return lax.dot_general(a, b, (((1,), (1,)), ((), ())), precision=precision,
                           preferred_element_type=F32)


def _sigmoid(x):
    return 1.0 / (1.0 + jnp.exp(-x))


def _layer_norm(y, g, b):
    mu = jnp.mean(y, axis=-1, keepdims=True)
    yc = y - mu
    var = jnp.mean(yc * yc, axis=-1, keepdims=True)
    return yc * lax.rsqrt(var + LN_EPS) * g + b


def _rope_table_kernel(pos_ref, invf_ref, cos_ref, sin_ref):
    ang = pos_ref[...].astype(F32) * invf_ref[...]
    cos_ref[...] = jnp.cos(ang)
    sin_ref[...] = jnp.sin(ang)


def _rope_tables(pos_col, invf_row):
    n = pos_col.shape[0]
    tm = 1024
    return pl.pallas_call(
        _rope_table_kernel,
        out_shape=(jax.ShapeDtypeStruct((n, LANES), F32), jax.ShapeDtypeStruct((n, LANES), F32)),
        grid=(n // tm,),
        in_specs=[pl.BlockSpec((tm, 1), lambda i: (i, 0)),
                  pl.BlockSpec((1, LANES), lambda i: (0, 0))],
        out_specs=(pl.BlockSpec((tm, LANES), lambda i: (i, 0)),
                   pl.BlockSpec((tm, LANES), lambda i: (i, 0))),
        compiler_params=pltpu.CompilerParams(dimension_semantics=("arbitrary",)),
        name="rope_tables",
    )(pos_col, invf_row)


def _qkv_kernel(x_ref, w_ref, cos_ref, sin_ref, sbq, sbk, sbv, daq, dak, dav):
    xb = x_ref[...].astype(BF16)
    lane = lax.broadcasted_iota(jnp.int32, (1, LANES), 1) % HEAD_DIM
    half = ROPE_DIM // 2
    cosr = jnp.where(lane < ROPE_DIM, cos_ref[...], 1.0)
    sin_lo = jnp.where(lane < half, -sin_ref[...], 0.0)
    sin_hi = jnp.where((lane >= half) & (lane < ROPE_DIM), sin_ref[...], 0.0)
    scale = HEAD_DIM ** -0.5
    outs = (sbq, sbk, sbv, daq, dak, dav)
    for c, o_ref in enumerate(outs):
        hc = _dot(xb, w_ref[:, c * SB_WIDTH:(c + 1) * SB_WIDTH])
        for blk in range(SB_WIDTH // LANES):
            h = hc[:, blk * LANES:(blk + 1) * LANES]
            if c in (3, 4):
                h = (h * cosr + pltpu.roll(h, LANES - half, 1) * sin_lo
                     + pltpu.roll(h, half, 1) * sin_hi)
            if c in (0, 3):
                h = h * scale
            o_ref[:, blk * LANES:(blk + 1) * LANES] = h.astype(BF16)


def _qkv(x2d, w_bf16, cos_t, sin_t, layer):
    n = x2d.shape[0]
    tm = TM_QKV
    row = lambda i: (i, 0)
    out_sd = jax.ShapeDtypeStruct((n, SB_WIDTH), BF16)
    return pl.pallas_call(
        _qkv_kernel,
        out_shape=(out_sd,) * 6,
        grid=(n // tm,),
        in_specs=[pl.BlockSpec((tm, D_MODEL), row),
                  pl.BlockSpec((None,) + w_bf16.shape[1:], lambda i: (layer, 0, 0)),
                  pl.BlockSpec((tm, LANES), row),
                  pl.BlockSpec((tm, LANES), row)],
        out_specs=(pl.BlockSpec((tm, SB_WIDTH), row),) * 6,
        compiler_params=pltpu.CompilerParams(dimension_semantics=("arbitrary",),
                                             vmem_limit_bytes=VMEM_LIMIT),
        name="qkv",
    )(x2d, w_bf16, cos_t, sin_t)


def _stack_halves(q):
    first = lax.broadcasted_iota(jnp.int32, (1, LANES), 1) < HEAD_DIM
    zero = jnp.zeros_like(q)
    return jnp.concatenate([jnp.where(first, q, zero), jnp.where(first, zero, q)], axis=0)


def _local_positions(rows, bq, bk):
    r = lax.broadcasted_iota(jnp.int32, (rows, bk), 0)
    return jnp.where(r >= bq, r - bq, r), lax.broadcasted_iota(jnp.int32, (rows, bk), 1)


def _sb_kernel(q_ref, k_ref, v_ref, u_ref, g_ref, o_ref):
    qi = pl.program_id(2)
    qs = _stack_halves(q_ref[...])
    rows = 2 * SB_BQ
    t_local, s_local = _local_positions(rows, SB_BQ, SB_BK)
    u = u_ref[...]

    def block(qm, kb, carry, acc, mask):
        start = pl.multiple_of(kb * SB_BK, SB_BK)
        ks = k_ref[pl.ds(start, SB_BK), :]
        vs = v_ref[pl.ds(start, SB_BK), :]
        z = _dot_nt(qm, ks)
        sp = jnp.maximum(z, 0.0) + jnp.log(1.0 + jnp.exp2(jnp.abs(z) * (-LOG2E)))
        if mask is not None:
            sp = jnp.where(mask, sp, 0.0)
        tail = _dot(sp.astype(BF16), u)
        a = jnp.exp(z - tail - carry)
        if mask is not None:
            a = jnp.where(mask, a, 0.0)
        acc = acc + _dot(a.astype(BF16), vs)
        carry = carry + jnp.sum(sp, axis=1, keepdims=True)
        return carry, acc

    carry, acc = jnp.zeros((rows, 1), F32), jnp.zeros((rows, LANES), F32)
    per_q = SB_BQ // SB_BK
    for j in reversed(range(per_q)):
        carry, acc = block(qs, qi * per_q + j, carry, acc, s_local + j * SB_BK < t_local)

    def more(st):
        return (st[0] >= 0) & (jnp.min(st[1]) < SB_UNDERFLOW)

    def step(st):
        carry, acc = block(qs, st[0], st[1], st[2], None)
        return st[0] - 1, carry, acc

    _, carry, acc = lax.while_loop(more, step, (qi * per_q - 1, carry, acc))

    first = lax.broadcasted_iota(jnp.int32, (1, LANES), 1) < HEAD_DIM
    o = jnp.where(first, acc[:SB_BQ], acc[SB_BQ:])
    sq = o * o
    ss0 = jnp.sum(jnp.where(first, sq, 0.0), axis=1, keepdims=True)
    ss1 = jnp.sum(jnp.where(first, 0.0, sq), axis=1, keepdims=True)
    ms = jnp.where(first, ss0, ss1) * (1.0 / HEAD_DIM)
    o_ref[...] = (o * lax.rsqrt(ms + LN_EPS) * g_ref[...]).astype(BF16)


def _sb_attention(q, k, v, u_tri, g_row, batch, seq):
    nq = seq // SB_BQ
    n_pairs = SB_WIDTH // LANES
    return pl.pallas_call(
        _sb_kernel,
        out_shape=jax.ShapeDtypeStruct(q.shape, BF16),
        grid=(batch, n_pairs, nq),
        in_specs=[pl.BlockSpec((SB_BQ, LANES), lambda b, h, i: (b * nq + i, h)),
                  pl.BlockSpec((seq, LANES), lambda b, h, i: (b, h)),
                  pl.BlockSpec((seq, LANES), lambda b, h, i: (b, h)),
                  pl.BlockSpec((SB_BK, SB_BK), lambda b, h, i: (0, 0)),
                  pl.BlockSpec((1, LANES), lambda b, h, i: (0, 0))],
        out_specs=pl.BlockSpec((SB_BQ, LANES), lambda b, h, i: (b * nq + i, h)),
        compiler_params=pltpu.CompilerParams(
            dimension_semantics=("arbitrary", "arbitrary", "arbitrary"), vmem_limit_bytes=VMEM_LIMIT),
        name="sb_attn",
    )(q, k, v, u_tri, g_row)


def _da_kernel(q_ref, k_ref, v_ref, lam_ref, g_ref, o_ref, *, lambda_init):
    qi = pl.program_id(2)
    qs = _stack_halves(q_ref[...])
    rows = 2 * DA_BQ
    t_local, s_local = _local_positions(rows, DA_BQ, DA_BK)

    def block(kb, m, l, acc, mask):
        start = pl.multiple_of(kb * DA_BK, DA_BK)
        ks = k_ref[pl.ds(start, DA_BK), :]
        vs = v_ref[pl.ds(start, DA_BK), :]
        s = _dot_nt(qs, ks)
        if mask is not None:
            s = jnp.where(mask, s, -jnp.inf)
        m_blk = jnp.max(s, axis=1, keepdims=True)
        m_new = m_blk if m is None else jnp.maximum(m, m_blk)
        p = jnp.exp(s - m_new)
        pv = _dot(p.astype(BF16), vs)
        ps = jnp.sum(p, axis=1, keepdims=True)
        if m is None:
            return m_new, ps, pv
        alpha = jnp.exp(m - m_new)
        return m_new, alpha * l + ps, alpha * acc + pv

    per_q = DA_BQ // DA_BK
    m = l = acc = None
    for j in range(per_q):
        m, l, acc = block(qi * per_q + j, m, l, acc, s_local + j * DA_BK <= t_local)
    m, l, acc = lax.fori_loop(0, qi * per_q, lambda kb, st: block(kb, st[0], st[1], st[2], None), (m, l, acc))
    o_maps = acc / l

    lp = lam_ref[...]
    lam = (jnp.exp(jnp.sum(lp[0:1] * lp[1:2], keepdims=True))
           - jnp.exp(jnp.sum(lp[2:3] * lp[3:4], keepdims=True)) + lambda_init)
    o = o_maps[:DA_BQ] - lam * o_maps[DA_BQ:]
    ms = jnp.mean(o * o, axis=1, keepdims=True)
    o_ref[...] = (o * lax.rsqrt(ms + LN_EPS) * (g_ref[...] * (1.0 - lambda_init))).astype(BF16)


def _da_attention(q, k, v, lam_p, g_row, batch, seq, lambda_init):
    nq = seq // DA_BQ
    return pl.pallas_call(
        functools.partial(_da_kernel, lambda_init=lambda_init),
        out_shape=jax.ShapeDtypeStruct(q.shape, BF16),
        grid=(batch, DA_HEADS, nq),
        in_specs=[pl.BlockSpec((DA_BQ, LANES), lambda b, h, i: (b * nq + i, h)),
                  pl.BlockSpec((seq, LANES), lambda b, h, i: (b, h)),
                  pl.BlockSpec((seq, LANES), lambda b, h, i: (b, h)),
                  pl.BlockSpec(lam_p.shape, lambda b, h, i: (0, 0)),
                  pl.BlockSpec((1, LANES), lambda b, h, i: (0, 0))],
        out_specs=pl.BlockSpec((DA_BQ, LANES), lambda b, h, i: (b * nq + i, h)),
        compiler_params=pltpu.CompilerParams(
            dimension_semantics=("arbitrary", "arbitrary", "arbitrary"), vmem_limit_bytes=VMEM_LIMIT),
        name="da_attn",
    )(q, k, v, lam_p, g_row)


def _route_kernel(sb_ref, da_ref, x_ref, wot_ref, wob_ref, g_ref, b_ref, rw_ref, rb_ref, u_ref,
                  xg_ref, cls_ref, rank_ref, cnt_ref, *, alpha):
    tm = x_ref.shape[0]

    @pl.when(pl.program_id(0) == 0)
    def _():
        cnt_ref[...] = jnp.zeros_like(cnt_ref)

    mix = _dot(sb_ref[...], wot_ref[...]) + _dot(da_ref[...], wob_ref[...])
    x1 = _layer_norm(alpha * x_ref[...] + mix, g_ref[...], b_ref[...])
    xg_ref[:, :D_MODEL] = x1

    x_hi = x1.astype(BF16)
    x_lo = (x1 - x_hi.astype(F32)).astype(BF16)
    by_hi = _dot_nt(rw_ref[...], x_hi)
    logits = (by_hi[0:ROUTER_ROWS] + by_hi[ROUTER_ROWS:]) + _dot_nt(rw_ref[0:ROUTER_ROWS], x_lo)
    scores = _sigmoid(logits)
    sel = scores + rb_ref[...]
    grow = lax.broadcasted_iota(jnp.int32, (8, tm), 0)
    a, b, c, d = sel[0:8], sel[8:16], sel[16:24], sel[24:32]
    hi1, lo1 = jnp.maximum(a, b), jnp.minimum(a, b)
    hi2, lo2 = jnp.maximum(c, d), jnp.minimum(c, d)
    top1 = jnp.maximum(hi1, hi2)
    top2 = jnp.maximum(jnp.minimum(hi1, hi2), jnp.maximum(lo1, lo2))
    gscore = jnp.where(grow < N_GROUPS, top1 + top2, -jnp.inf)
    gmax = jnp.max(gscore, axis=0, keepdims=True)
    grp = jnp.min(jnp.where(gscore == gmax, grow, 8), axis=0, keepdims=True)
    gsel = grow == grp

    one = lambda m: jnp.where(m, 1, 0)
    ranks = (one(b > a) + one(c > a) + one(d > a),
             one(a >= b) + one(c > b) + one(d > b),
             one(a >= c) + one(b >= c) + one(d > c),
             one(a >= d) + one(b >= d) + one(c >= d))
    sc = (scores[0:8], scores[8:16], scores[16:24], scores[24:32])
    chosen = [jnp.where(gsel, one(r < TOP_K), 0) for r in ranks]
    picked = [jnp.sum(ch, axis=0, keepdims=True) for ch in chosen]
    num = [jnp.sum(jnp.where(ch == 1, s, 0.0), axis=0, keepdims=True) for ch, s in zip(chosen, sc)]
    den = num[0] + num[1] + num[2] + num[3]
    gate = [n_ / den for n_ in num]
    g_first = jnp.where(picked[0] == 1, gate[0], jnp.where(picked[1] == 1, gate[1], gate[2]))
    g_second = jnp.where(picked[3] == 1, gate[3], jnp.where(picked[2] == 1, gate[2], gate[1]))
    code = picked[0] + 2 * picked[1] + 4 * picked[2] + 8 * picked[3]
    pair = jnp.where(code == 3, 0, jnp.where(code == 5, 1, jnp.where(code == 6, 2,
           jnp.where(code == 9, 3, jnp.where(code == 10, 4, 5)))))
    cls = grp * N_PAIRS + pair

    crow = lax.broadcasted_iota(jnp.int32, (CLASS_ROWS, tm), 0)
    onehot = jnp.where(crow == cls, 1.0, 0.0)
    prefix = _dot(onehot.astype(BF16), u_ref[...])
    before = cnt_ref[:, 0:1]
    rank = jnp.sum(onehot * (prefix + before), axis=0, keepdims=True) - 1.0
    cnt_ref[...] = cnt_ref[...] + jnp.sum(onehot, axis=1, keepdims=True)
    cls_ref[0] = cls
    rank_ref[0] = rank.astype(jnp.int32)

    trow = lax.broadcasted_iota(jnp.int32, (GATE_COLS, tm), 0)
    gates_t = jnp.where(trow == 0, g_first, jnp.where(trow == 1, g_second, 0.0))
    xg_ref[:, D_MODEL:] = gates_t.T


TOP_K = 2


def _proj_route(sb, da, x2d, wo_top, wo_bot, g, b, rw_t, rb_col, u_tri, alpha):
    n = x2d.shape[0]
    tm = TM_ROUTE
    nt = n // tm
    row = lambda i: (i, 0)
    const = lambda i: (0, 0)
    return pl.pallas_call(
        functools.partial(_route_kernel, alpha=alpha),
        out_shape=(jax.ShapeDtypeStruct((n, ROW_W), F32),
                   jax.ShapeDtypeStruct((nt, 1, tm), jnp.int32),
                   jax.ShapeDtypeStruct((nt, 1, tm), jnp.int32),
                   jax.ShapeDtypeStruct((CLASS_ROWS, LANES), F32)),
        grid=(nt,),
        in_specs=[pl.BlockSpec((tm, SB_WIDTH), row), pl.BlockSpec((tm, DA_WIDTH), row),
                  pl.BlockSpec((tm, D_MODEL), row),
                  pl.BlockSpec(wo_top.shape, const), pl.BlockSpec(wo_bot.shape, const),
                  pl.BlockSpec((1, D_MODEL), const), pl.BlockSpec((1, D_MODEL), const),
                  pl.BlockSpec(rw_t.shape, const), pl.BlockSpec(rb_col.shape, const),
                  pl.BlockSpec((tm, tm), const)],
        out_specs=(pl.BlockSpec((tm, ROW_W), row),
                   pl.BlockSpec((1, 1, tm), lambda i: (i, 0, 0)),
                   pl.BlockSpec((1, 1, tm), lambda i: (i, 0, 0)),
                   pl.BlockSpec((CLASS_ROWS, LANES), const)),
        compiler_params=pltpu.CompilerParams(dimension_semantics=("arbitrary",),
                                             vmem_limit_bytes=VMEM_LIMIT),
        name="proj_route",
    )(sb, da, x2d, wo_top, wo_bot, g, b, rw_t, rb_col, u_tri)


def _dispatch_kernel(pos_ref, xg_ref, zeros_ref, xs_ref, sem):
    del zeros_ref
    tm = xg_ref.shape[0]
    base = pl.program_id(0) * tm

    def issue(g, _):
        for k in range(ROW_UNROLL):
            r = g * ROW_UNROLL + k
            pltpu.make_async_copy(xg_ref.at[pl.ds(r, 1), :],
                                  xs_ref.at[pl.ds(pos_ref[base + r], 1), :], sem).start()
        return 0

    lax.fori_loop(0, tm // ROW_UNROLL, issue, 0)
    pltpu.make_async_copy(xg_ref, xs_ref.at[pl.ds(0, tm), :], sem).wait()


def _dispatch(pos, xg, n_rows):
    n = xg.shape[0]
    tm = TM_DISPATCH
    zeros = jnp.zeros((n_rows, ROW_W), F32)
    return pl.pallas_call(
        _dispatch_kernel,
        out_shape=jax.ShapeDtypeStruct((n_rows, ROW_W), F32),
        grid_spec=pltpu.PrefetchScalarGridSpec(
            num_scalar_prefetch=1,
            grid=(n // tm,),
            in_specs=[pl.BlockSpec((tm, ROW_W), lambda i, pos: (i, 0)),
                      pl.BlockSpec(memory_space=pl.ANY)],
            out_specs=pl.BlockSpec(memory_space=pl.ANY),
            scratch_shapes=[pltpu.SemaphoreType.DMA]),
        input_output_aliases={2: 0},
        compiler_params=pltpu.CompilerParams(dimension_semantics=("arbitrary",),
                                             vmem_limit_bytes=VMEM_LIMIT),
        name="dispatch",
    )(pos, xg, zeros)


def _moe_kernel(e1_ref, e2_ref, nv_ref, xs_ref, wg1, wu1, wd1, wg2, wu2, wd2, ys_ref, *, alpha):
    @pl.when(pl.program_id(0) < nv_ref[0])
    def _():
        x = xs_ref[:, :D_MODEL]
        gates = xs_ref[:, D_MODEL:]
        xb = x.astype(BF16)

        def expert(wg, wu, wd):
            gt = _dot(xb, wg[...])
            h = gt * _sigmoid(gt) * _dot(xb, wu[...])
            return _dot(h.astype(BF16), wd[...])

        ffn = gates[:, 0:1] * expert(wg1, wu1, wd1) + gates[:, 1:2] * expert(wg2, wu2, wd2)
        ys_ref[...] = alpha * x + ffn

    @pl.when(pl.program_id(0) >= nv_ref[0])
    def _():
        ys_ref[...] = jnp.zeros_like(ys_ref)


def _moe(tile_e1, tile_e2, n_valid, xs, wg, wu, wd, layer, alpha):
    n_rows = xs.shape[0]
    n_tiles = n_rows // MOE_TILE
    tile = lambda i, e1, e2, nv: (jnp.maximum(jnp.minimum(i, nv[0] - 1), 0), 0)
    first = lambda i, e1, e2, nv: (layer, e1[i], 0, 0)
    second = lambda i, e1, e2, nv: (layer, e2[i], 0, 0)
    up_spec = lambda f: pl.BlockSpec((None, None, D_MODEL, D_FF), f)
    down_spec = lambda f: pl.BlockSpec((None, None, D_FF, D_MODEL), f)
    return pl.pallas_call(
        functools.partial(_moe_kernel, alpha=alpha),
        out_shape=jax.ShapeDtypeStruct((n_rows, D_MODEL), F32),
        grid_spec=pltpu.PrefetchScalarGridSpec(
            num_scalar_prefetch=3,
            grid=(n_tiles,),
            in_specs=[pl.BlockSpec((MOE_TILE, ROW_W), tile),
                      up_spec(first), up_spec(first), down_spec(first),
                      up_spec(second), up_spec(second), down_spec(second)],
            out_specs=pl.BlockSpec((MOE_TILE, D_MODEL), lambda i, e1, e2, nv: (i, 0))),
        compiler_params=pltpu.CompilerParams(dimension_semantics=("arbitrary",),
                                             vmem_limit_bytes=VMEM_LIMIT),
        name="moe",
    )(tile_e1, tile_e2, n_valid, xs, wg, wu, wd, wg, wu, wd)


def _combine_kernel(pos_ref, ys_ref, p_ref, g_ref, b_ref, wpg_ref, bpg_ref, wple_ref, o_ref, buf, sem):
    tm = o_ref.shape[0]
    i = pl.program_id(0)
    slot = i % 2

    last = pl.num_programs(0) - 1

    def row_copy(tile, r, dst_slot):
        return pltpu.make_async_copy(ys_ref.at[pl.ds(pos_ref[tile * tm + r], 1), :],
                                     buf.at[dst_slot, pl.ds(r, 1), :], sem.at[dst_slot])

    def drain(s):
        pltpu.make_async_copy(ys_ref.at[pl.ds(0, tm), :], buf.at[s], sem.at[s]).wait()

    @pl.when(i == 0)
    def _():
        def issue(g, _):
            for k in range(ROW_UNROLL):
                row_copy(0, g * ROW_UNROLL + k, 0).start()
            return 0
        lax.fori_loop(0, tm // ROW_UNROLL, issue, 0)

    drain(slot)
    xin = buf[slot]
    nxt = jnp.minimum(i + 1, last)
    for r in range(tm):
        row_copy(nxt, r, 1 - slot).start()
    x2 = _layer_norm(xin, g_ref[...], b_ref[...])
    gate = _sigmoid(_dot(x2.astype(BF16), wpg_ref[...]) + bpg_ref[...])
    o_ref[...] = x2 + gate * _dot(p_ref[...].astype(BF16), wple_ref[...])

    @pl.when(i == last)
    def _():
        drain(1 - slot)


def _combine_ple(pos, ys, p3d, g, b, wpg, bpg, wple, layer):
    n = p3d.shape[1]
    tm = TM_ROWS
    row = lambda i, pos: (i, 0)
    const = lambda i, pos: (0, 0)
    layer_const = lambda i, pos: (layer, 0, 0)
    return pl.pallas_call(
        _combine_kernel,
        out_shape=jax.ShapeDtypeStruct((n, D_MODEL), F32),
        grid_spec=pltpu.PrefetchScalarGridSpec(
            num_scalar_prefetch=1,
            grid=(n // tm,),
            in_specs=[pl.BlockSpec(memory_space=pl.ANY),
                      pl.BlockSpec((None, tm, PLE_DIM), lambda i, pos: (layer, i, 0)),
                      pl.BlockSpec((1, D_MODEL), const), pl.BlockSpec((1, D_MODEL), const),
                      pl.BlockSpec((None,) + wpg.shape[1:], layer_const),
                      pl.BlockSpec((1, D_MODEL), const),
                      pl.BlockSpec((None,) + wple.shape[1:], layer_const)],
            out_specs=pl.BlockSpec((tm, D_MODEL), row),
            scratch_shapes=[pltpu.VMEM((2, tm, D_MODEL), F32), pltpu.SemaphoreType.DMA((2,))]),
        compiler_params=pltpu.CompilerParams(dimension_semantics=("arbitrary",),
                                             vmem_limit_bytes=VMEM_LIMIT),
        name="combine_ple",
    )(pos, ys, p3d, g, b, wpg, bpg, wple)


def _tri(n, inclusive_lower):
    j = np.arange(n)[:, None]
    s = np.arange(n)[None, :]
    return jnp.asarray((j >= s) if inclusive_lower else (j <= s), dtype=BF16)


def kernel(x, p, positions, w_in, w_o, sb_norm_g, da_lambda, da_subln_g, ln1_g, ln1_b, ln2_g, ln2_b,
           router_w, router_b, w_gate, w_up, w_down, w_ple, w_ple_gate, b_ple_gate):
    batch, seq, d = x.shape
    depth = w_in.shape[0]
    n = batch * seq
    alpha = (2 * depth) ** 0.25
    max_tiles = n // MOE_TILE + N_CLASSES
    n_rows = max_tiles * MOE_TILE

    inv_freq = ROPE_THETA ** (-jnp.arange(0, ROPE_DIM, 2, dtype=F32) / ROPE_DIM)
    invf_row = jnp.tile(jnp.concatenate([inv_freq, inv_freq, jnp.zeros((HEAD_DIM - ROPE_DIM,), F32)]),
                        LANES // HEAD_DIM)[None, :]
    cos_t, sin_t = _rope_tables(positions.reshape(n, 1), invf_row)

    src = np.array([[4 * g + l for g in range(N_GROUPS)] for l in range(EXPERTS_PER_GROUP)])
    rw_t = jnp.zeros((EXPERTS_PER_GROUP, 8, d), F32).at[:, :N_GROUPS, :].set(router_w.T[src])
    rw_t = rw_t.reshape(ROUTER_ROWS, d)
    rw_hi = rw_t.astype(BF16)
    rw_pad = jnp.concatenate([rw_hi, (rw_t - rw_hi.astype(F32)).astype(BF16)], axis=0)
    rb_col = jnp.zeros((EXPERTS_PER_GROUP, 8), F32).at[:, :N_GROUPS].set(router_b[src]).reshape(ROUTER_ROWS, 1)

    u_sb = _tri(SB_BK, True)
    u_rank = _tri(TM_ROUTE, False)
    class_e1 = jnp.asarray(_CLASS_E1)
    class_e2 = jnp.asarray(_CLASS_E2)

    w_in_b, wg_b, wu_b, wd_b = (w.astype(BF16) for w in (w_in, w_gate, w_up, w_down))
    wpg_b, wple_b = w_ple_gate.astype(BF16), w_ple.astype(BF16)
    p3d = p.reshape(depth, n, PLE_DIM)

    x2d = x.reshape(n, d)
    for i in range(depth):
        lambda_init = 0.8 - 0.6 * math.exp(-0.3 * i)
        sbq, sbk, sbv, daq, dak, dav = _qkv(x2d, w_in_b, cos_t, sin_t, i)
        g_sb = jnp.tile(sb_norm_g[i], LANES // HEAD_DIM)[None, :]
        sb = _sb_attention(sbq, sbk, sbv, u_sb, g_sb, batch, seq)
        da = _da_attention(daq, dak, dav, da_lambda[i], da_subln_g[i][None, :], batch, seq, lambda_init)

        wo = w_o[i].astype(BF16)
        xg, cls, rank, counts = _proj_route(sb, da, x2d, wo[:SB_WIDTH], wo[SB_WIDTH:],
                                            ln1_g[i][None, :], ln1_b[i][None, :], rw_pad, rb_col, u_rank, alpha)

        cnt = counts[:N_CLASSES, 0].astype(jnp.int32)
        tiles_c = (cnt + MOE_TILE - 1) // MOE_TILE
        tile_end = jnp.cumsum(tiles_c)
        row_off = (tile_end - tiles_c) * MOE_TILE
        pos = row_off[cls.reshape(n)] + rank.reshape(n)
        n_valid = tile_end[-1:]
        tile_ids = jnp.minimum(jnp.arange(max_tiles, dtype=jnp.int32), n_valid[0] - 1)
        tile_cls = jnp.minimum(jnp.sum((tile_ids[:, None] >= tile_end[None, :]).astype(jnp.int32), axis=1),
                               N_CLASSES - 1)

        xs = _dispatch(pos, xg, n_rows)
        ys = _moe(class_e1[tile_cls], class_e2[tile_cls], n_valid.astype(jnp.int32), xs,
                  wg_b, wu_b, wd_b, i, alpha)
        x2d = _combine_ple(pos, ys, p3d, ln2_g[i][None, :], ln2_b[i][None, :],
                           wpg_b, b_ple_gate[i][None, :], wple_b, i)
    return x2d.reshape(batch, seq, d)
```

```python
import functools
import math

import numpy as np
import jax
import jax.numpy as jnp
from jax import lax
from jax.experimental import pallas as pl
from jax.experimental.pallas import tpu as pltpu

D_MODEL = 1024
HEAD_DIM = 64
SB_WIDTH = 512
DA_WIDTH = 512
DA_HEADS = 4
ROPE_DIM = 16
ROPE_THETA = 500000.0
N_EXPERTS = 16
N_GROUPS = 4
EXPERTS_PER_GROUP = 4
D_FF = 512
PLE_DIM = 256
LN_EPS = 1e-5

LANES = 128
N_PAIRS = 6
N_CLASSES = N_GROUPS * N_PAIRS
CLASS_ROWS = 32
ROUTER_ROWS = 4 * 8
ROW_UNROLL = 8
GATE_COLS = LANES
ROW_W = D_MODEL + GATE_COLS

MOE_TILE = 256
SB_BQ = 512
SB_BK = 256
DA_BQ = 1024
DA_BK = 512
DA_BK_WIDE = 1024
SB_UNDERFLOW = 104.0
LOG2E = 1.4426950408889634
TM_QKV = 512
TM_ROUTE = 512
TM_DISPATCH = 2048
TM_ROWS = 512
VMEM_LIMIT = 56 * 1024 * 1024

F32 = jnp.float32
BF16 = jnp.bfloat16

_PAIR_LOCAL = [(0, 1), (0, 2), (1, 2), (0, 3), (1, 3), (2, 3)]
_CLASS_E1 = np.array([g * 4 + _PAIR_LOCAL[p][0] for g in range(N_GROUPS) for p in range(N_PAIRS)], np.int32)
_CLASS_E2 = np.array([g * 4 + _PAIR_LOCAL[p][1] for g in range(N_GROUPS) for p in range(N_PAIRS)], np.int32)


def _dot(a, b):
    return jnp.dot(a, b, preferred_element_type=F32)


def _dot_nt(a, b, precision=None):
    return lax.dot_general(a, b, (((1,), (1,)), ((), ())), precision=precision,
                           preferred_element_type=F32)


def _sigmoid(x):
    return 1.0 / (1.0 + jnp.exp(-x))


def _layer_norm(y, g, b):
    mu = jnp.mean(y, axis=-1, keepdims=True)
    yc = y - mu
    var = jnp.mean(yc * yc, axis=-1, keepdims=True)
    return yc * lax.rsqrt(var + LN_EPS) * g + b


def _rope_table_kernel(pos_ref, invf_ref, cos_ref, sin_ref):
    ang = pos_ref[...].astype(F32) * invf_ref[...]
    cos_ref[...] = jnp.cos(ang)
    sin_ref[...] = jnp.sin(ang)


def _rope_tables(pos_col, invf_row):
    n = pos_col.shape[0]
    tm = 1024
    return pl.pallas_call(
        _rope_table_kernel,
        out_shape=(jax.ShapeDtypeStruct((n, LANES), F32), jax.ShapeDtypeStruct((n, LANES), F32)),
        grid=(n // tm,),
        in_specs=[pl.BlockSpec((tm, 1), lambda i: (i, 0)),
                  pl.BlockSpec((1, LANES), lambda i: (0, 0))],
        out_specs=(pl.BlockSpec((tm, LANES), lambda i: (i, 0)),
                   pl.BlockSpec((tm, LANES), lambda i: (i, 0))),
        compiler_params=pltpu.CompilerParams(dimension_semantics=("arbitrary",)),
        name="rope_tables",
    )(pos_col, invf_row)


def _qkv_kernel(x_ref, w_ref, cos_ref, sin_ref, sbq, sbk, sbv, daq, dak, dav):
    xb = x_ref[...].astype(BF16)
    lane = lax.broadcasted_iota(jnp.int32, (1, LANES), 1) % HEAD_DIM
    half = ROPE_DIM // 2
    cosr = jnp.where(lane < ROPE_DIM, cos_ref[...], 1.0)
    sin_lo = jnp.where(lane < half, -sin_ref[...], 0.0)
    sin_hi = jnp.where((lane >= half) & (lane < ROPE_DIM), sin_ref[...], 0.0)
    scale = HEAD_DIM ** -0.5
    outs = (sbq, sbk, sbv, daq, dak, dav)
    for c, o_ref in enumerate(outs):
        hc = _dot(xb, w_ref[:, c * SB_WIDTH:(c + 1) * SB_WIDTH])
        for blk in range(SB_WIDTH // LANES):
            h = hc[:, blk * LANES:(blk + 1) * LANES]
            if c in (3, 4):
                h = (h * cosr + pltpu.roll(h, LANES - half, 1) * sin_lo
                     + pltpu.roll(h, half, 1) * sin_hi)
            if c in (0, 3):
                h = h * scale
            o_ref[:, blk * LANES:(blk + 1) * LANES] = h.astype(BF16)


def _qkv(x2d, w_bf16, cos_t, sin_t, layer):
    n = x2d.shape[0]
    tm = TM_QKV
    row = lambda i: (i, 0)
    out_sd = jax.ShapeDtypeStruct((n, SB_WIDTH), BF16)
    return pl.pallas_call(
        _qkv_kernel,
        out_shape=(out_sd,) * 6,
        grid=(n // tm,),
        in_specs=[pl.BlockSpec((tm, D_MODEL), row),
                  pl.BlockSpec((None,) + w_bf16.shape[1:], lambda i: (layer, 0, 0)),
                  pl.BlockSpec((tm, LANES), row),
                  pl.BlockSpec((tm, LANES), row)],
        out_specs=(pl.BlockSpec((tm, SB_WIDTH), row),) * 6,
        compiler_params=pltpu.CompilerParams(dimension_semantics=("arbitrary",),
                                             vmem_limit_bytes=VMEM_LIMIT),
        name="qkv",
    )(x2d, w_bf16, cos_t, sin_t)


def _stack_halves(q):
    first = lax.broadcasted_iota(jnp.int32, (1, LANES), 1) < HEAD_DIM
    zero = jnp.zeros_like(q)
    return jnp.concatenate([jnp.where(first, q, zero), jnp.where(first, zero, q)], axis=0)


def _local_positions(rows, bq, bk):
    r = lax.broadcasted_iota(jnp.int32, (rows, bk), 0)
    return jnp.where(r >= bq, r - bq, r), lax.broadcasted_iota(jnp.int32, (rows, bk), 1)


def _sb_kernel(q_ref, k_ref, v_ref, u_ref, g_ref, o_ref):
    qi = pl.program_id(2)
    qs = _stack_halves(q_ref[...])
    rows = 2 * SB_BQ
    t_local, s_local = _local_positions(rows, SB_BQ, SB_BK)
    u = u_ref[...]

    def block(qm, kb, carry, acc, mask):
        start = pl.multiple_of(kb * SB_BK, SB_BK)
        ks = k_ref[pl.ds(start, SB_BK), :]
        vs = v_ref[pl.ds(start, SB_BK), :]
        z = _dot_nt(qm, ks)
        sp = jnp.maximum(z, 0.0) + jnp.log(1.0 + jnp.exp2(jnp.abs(z) * (-LOG2E)))
        if mask is not None:
            sp = jnp.where(mask, sp, 0.0)
        tail = _dot(sp.astype(BF16), u)
        a = jnp.exp(z - tail - carry)
        if mask is not None:
            a = jnp.where(mask, a, 0.0)
        acc = acc + _dot(a.astype(BF16), vs)
        carry = carry + jnp.sum(sp, axis=1, keepdims=True)
        return carry, acc

    carry, acc = jnp.zeros((rows, 1), F32), jnp.zeros((rows, LANES), F32)
    per_q = SB_BQ // SB_BK
    for j in reversed(range(per_q)):
        carry, acc = block(qs, qi * per_q + j, carry, acc, s_local + j * SB_BK < t_local)

    def more(st):
        return (st[0] >= 0) & (jnp.min(st[1]) < SB_UNDERFLOW)

    def step(st):
        carry, acc = block(qs, st[0], st[1], st[2], None)
        return st[0] - 1, carry, acc

    _, carry, acc = lax.while_loop(more, step, (qi * per_q - 1, carry, acc))

    first = lax.broadcasted_iota(jnp.int32, (1, LANES), 1) < HEAD_DIM
    o = jnp.where(first, acc[:SB_BQ], acc[SB_BQ:])
    sq = o * o
    ss0 = jnp.sum(jnp.where(first, sq, 0.0), axis=1, keepdims=True)
    ss1 = jnp.sum(jnp.where(first, 0.0, sq), axis=1, keepdims=True)
    ms = jnp.where(first, ss0, ss1) * (1.0 / HEAD_DIM)
    o_ref[...] = (o * lax.rsqrt(ms + LN_EPS) * g_ref[...]).astype(BF16)


def _sb_attention(q, k, v, u_tri, g_row, batch, seq):
    nq = seq // SB_BQ
    n_pairs = SB_WIDTH // LANES
    return pl.pallas_call(
        _sb_kernel,
        out_shape=jax.ShapeDtypeStruct(q.shape, BF16),
        grid=(batch, n_pairs, nq),
        in_specs=[pl.BlockSpec((SB_BQ, LANES), lambda b, h, i: (b * nq + i, h)),
                  pl.BlockSpec((seq, LANES), lambda b, h, i: (b, h)),
                  pl.BlockSpec((seq, LANES), lambda b, h, i: (b, h)),
                  pl.BlockSpec((SB_BK, SB_BK), lambda b, h, i: (0, 0)),
                  pl.BlockSpec((1, LANES), lambda b, h, i: (0, 0))],
        out_specs=pl.BlockSpec((SB_BQ, LANES), lambda b, h, i: (b * nq + i, h)),
        compiler_params=pltpu.CompilerParams(
            dimension_semantics=("arbitrary", "arbitrary", "arbitrary"), vmem_limit_bytes=VMEM_LIMIT),
        name="sb_attn",
    )(q, k, v, u_tri, g_row)


def _da_kernel(q_ref, k_ref, v_ref, lam_ref, g_ref, o_ref, *, lambda_init):
    qi = pl.program_id(2)
    qs = _stack_halves(q_ref[...])
    rows = 2 * DA_BQ
    t_local, s_local = _local_positions(rows, DA_BQ, DA_BK)

    def block(kb, m, l, acc, mask, bk=DA_BK):
        start = pl.multiple_of(kb * bk, bk)
        ks = k_ref[pl.ds(start, bk), :]
        vs = v_ref[pl.ds(start, bk), :]
        s = _dot_nt(qs, ks)
        if mask is not None:
            s = jnp.where(mask, s, -jnp.inf)
        m_blk = jnp.max(s, axis=1, keepdims=True)
        m_new = m_blk if m is None else jnp.maximum(m, m_blk)
        p = jnp.exp(s - m_new)
        pv = _dot(p.astype(BF16), vs)
        ps = jnp.sum(p, axis=1, keepdims=True)
        if m is None:
            return m_new, ps, pv
        alpha = jnp.exp(m - m_new)
        return m_new, alpha * l + ps, alpha * acc + pv

    per_q = DA_BQ // DA_BK
    m = l = acc = None
    for j in range(per_q):
        m, l, acc = block(qi * per_q + j, m, l, acc, s_local + j * DA_BK <= t_local)
    wide = DA_BQ // DA_BK_WIDE
    m, l, acc = lax.fori_loop(0, qi * wide,
                              lambda kb, st: block(kb, st[0], st[1], st[2], None, DA_BK_WIDE), (m, l, acc))
    o_maps = acc / l

    lp = lam_ref[...]
    lam = (jnp.exp(jnp.sum(lp[0:1] * lp[1:2], keepdims=True))
           - jnp.exp(jnp.sum(lp[2:3] * lp[3:4], keepdims=True)) + lambda_init)
    o = o_maps[:DA_BQ] - lam * o_maps[DA_BQ:]
    ms = jnp.mean(o * o, axis=1, keepdims=True)
    o_ref[...] = (o * lax.rsqrt(ms + LN_EPS) * (g_ref[...] * (1.0 - lambda_init))).astype(BF16)


def _da_attention(q, k, v, lam_p, g_row, batch, seq, lambda_init):
    nq = seq // DA_BQ
    return pl.pallas_call(
        functools.partial(_da_kernel, lambda_init=lambda_init),
        out_shape=jax.ShapeDtypeStruct(q.shape, BF16),
        grid=(batch, DA_HEADS, nq),
        in_specs=[pl.BlockSpec((DA_BQ, LANES), lambda b, h, i: (b * nq + i, h)),
                  pl.BlockSpec((seq, LANES), lambda b, h, i: (b, h)),
                  pl.BlockSpec((seq, LANES), lambda b, h, i: (b, h)),
                  pl.BlockSpec(lam_p.shape, lambda b, h, i: (0, 0)),
                  pl.BlockSpec((1, LANES), lambda b, h, i: (0, 0))],
        out_specs=pl.BlockSpec((DA_BQ, LANES), lambda b, h, i: (b * nq + i, h)),
        compiler_params=pltpu.CompilerParams(
            dimension_semantics=("arbitrary", "arbitrary", "arbitrary"), vmem_limit_bytes=VMEM_LIMIT),
        name="da_attn",
    )(q, k, v, lam_p, g_row)


def _route_kernel(sb_ref, da_ref, x_ref, wot_ref, wob_ref, g_ref, b_ref, rw_ref, rb_ref, u_ref,
                  xg_ref, cls_ref, rank_ref, cnt_ref, *, alpha):
    tm = x_ref.shape[0]

    @pl.when(pl.program_id(0) == 0)
    def _():
        cnt_ref[...] = jnp.zeros_like(cnt_ref)

    mix = _dot(sb_ref[...], wot_ref[...]) + _dot(da_ref[...], wob_ref[...])
    x1 = _layer_norm(alpha * x_ref[...] + mix, g_ref[...], b_ref[...])
    xg_ref[:, :D_MODEL] = x1

    x_hi = x1.astype(BF16)
    x_lo = (x1 - x_hi.astype(F32)).astype(BF16)
    by_hi = _dot_nt(rw_ref[...], x_hi)
    logits = (by_hi[0:ROUTER_ROWS] + by_hi[ROUTER_ROWS:]) + _dot_nt(rw_ref[0:ROUTER_ROWS], x_lo)
    scores = _sigmoid(logits)
    sel = scores + rb_ref[...]
    grow = lax.broadcasted_iota(jnp.int32, (8, tm), 0)
    a, b, c, d = sel[0:8], sel[8:16], sel[16:24], sel[24:32]
    hi1, lo1 = jnp.maximum(a, b), jnp.minimum(a, b)
    hi2, lo2 = jnp.maximum(c, d), jnp.minimum(c, d)
    top1 = jnp.maximum(hi1, hi2)
    top2 = jnp.maximum(jnp.minimum(hi1, hi2), jnp.maximum(lo1, lo2))
    gscore = jnp.where(grow < N_GROUPS, top1 + top2, -jnp.inf)
    gmax = jnp.max(gscore, axis=0, keepdims=True)
    grp = jnp.min(jnp.where(gscore == gmax, grow, 8), axis=0, keepdims=True)
    gsel = grow == grp

    one = lambda m: jnp.where(m, 1, 0)
    ranks = (one(b > a) + one(c > a) + one(d > a),
             one(a >= b) + one(c > b) + one(d > b),
             one(a >= c) + one(b >= c) + one(d > c),
             one(a >= d) + one(b >= d) + one(c >= d))
    sc = (scores[0:8], scores[8:16], scores[16:24], scores[24:32])
    chosen = [jnp.where(gsel, one(r < TOP_K), 0) for r in ranks]
    picked = [jnp.sum(ch, axis=0, keepdims=True) for ch in chosen]
    num = [jnp.sum(jnp.where(ch == 1, s, 0.0), axis=0, keepdims=True) for ch, s in zip(chosen, sc)]
    den = num[0] + num[1] + num[2] + num[3]
    gate = [n_ / den for n_ in num]
    g_first = jnp.where(picked[0] == 1, gate[0], jnp.where(picked[1] == 1, gate[1], gate[2]))
    g_second = jnp.where(picked[3] == 1, gate[3], jnp.where(picked[2] == 1, gate[2], gate[1]))
    code = picked[0] + 2 * picked[1] + 4 * picked[2] + 8 * picked[3]
    pair = jnp.where(code == 3, 0, jnp.where(code == 5, 1, jnp.where(code == 6, 2,
           jnp.where(code == 9, 3, jnp.where(code == 10, 4, 5)))))
    cls = grp * N_PAIRS + pair

    crow = lax.broadcasted_iota(jnp.int32, (CLASS_ROWS, tm), 0)
    onehot = jnp.where(crow == cls, 1.0, 0.0)
    prefix = _dot(onehot.astype(BF16), u_ref[...])
    before = cnt_ref[:, 0:1]
    rank = jnp.sum(onehot * (prefix + before), axis=0, keepdims=True) - 1.0
    cnt_ref[...] = cnt_ref[...] + jnp.sum(onehot, axis=1, keepdims=True)
    cls_ref[0] = cls
    rank_ref[0] = rank.astype(jnp.int32)

    trow = lax.broadcasted_iota(jnp.int32, (GATE_COLS, tm), 0)
    gates_t = jnp.where(trow == 0, g_first, jnp.where(trow == 1, g_second, 0.0))
    xg_ref[:, D_MODEL:] = gates_t.T


TOP_K = 2


def _proj_route(sb, da, x2d, wo_top, wo_bot, g, b, rw_t, rb_col, u_tri, alpha):
    n = x2d.shape[0]
    tm = TM_ROUTE
    nt = n // tm
    row = lambda i: (i, 0)
    const = lambda i: (0, 0)
    return pl.pallas_call(
        functools.partial(_route_kernel, alpha=alpha),
        out_shape=(jax.ShapeDtypeStruct((n, ROW_W), F32),
                   jax.ShapeDtypeStruct((nt, 1, tm), jnp.int32),
                   jax.ShapeDtypeStruct((nt, 1, tm), jnp.int32),
                   jax.ShapeDtypeStruct((CLASS_ROWS, LANES), F32)),
        grid=(nt,),
        in_specs=[pl.BlockSpec((tm, SB_WIDTH), row), pl.BlockSpec((tm, DA_WIDTH), row),
                  pl.BlockSpec((tm, D_MODEL), row),
                  pl.BlockSpec(wo_top.shape, const), pl.BlockSpec(wo_bot.shape, const),
                  pl.BlockSpec((1, D_MODEL), const), pl.BlockSpec((1, D_MODEL), const),
                  pl.BlockSpec(rw_t.shape, const), pl.BlockSpec(rb_col.shape, const),
                  pl.BlockSpec((tm, tm), const)],
        out_specs=(pl.BlockSpec((tm, ROW_W), row),
                   pl.BlockSpec((1, 1, tm), lambda i: (i, 0, 0)),
                   pl.BlockSpec((1, 1, tm), lambda i: (i, 0, 0)),
                   pl.BlockSpec((CLASS_ROWS, LANES), const)),
        compiler_params=pltpu.CompilerParams(dimension_semantics=("arbitrary",),
                                             vmem_limit_bytes=VMEM_LIMIT),
        name="proj_route",
    )(sb, da, x2d, wo_top, wo_bot, g, b, rw_t, rb_col, u_tri)


def _dispatch_kernel(pos_ref, xg_ref, zeros_ref, xs_ref, sem):
    del zeros_ref
    tm = xg_ref.shape[0]
    base = pl.program_id(0) * tm

    def issue(g, _):
        for k in range(ROW_UNROLL):
            r = g * ROW_UNROLL + k
            pltpu.make_async_copy(xg_ref.at[pl.ds(r, 1), :],
                                  xs_ref.at[pl.ds(pos_ref[base + r], 1), :], sem).start()
        return 0

    lax.fori_loop(0, tm // ROW_UNROLL, issue, 0)
    pltpu.make_async_copy(xg_ref, xs_ref.at[pl.ds(0, tm), :], sem).wait()


def _dispatch(pos, xg, n_rows):
    n = xg.shape[0]
    tm = TM_DISPATCH
    zeros = jnp.zeros((n_rows, ROW_W), F32)
    return pl.pallas_call(
        _dispatch_kernel,
        out_shape=jax.ShapeDtypeStruct((n_rows, ROW_W), F32),
        grid_spec=pltpu.PrefetchScalarGridSpec(
            num_scalar_prefetch=1,
            grid=(n // tm,),
            in_specs=[pl.BlockSpec((tm, ROW_W), lambda i, pos: (i, 0)),
                      pl.BlockSpec(memory_space=pl.ANY)],
            out_specs=pl.BlockSpec(memory_space=pl.ANY),
            scratch_shapes=[pltpu.SemaphoreType.DMA]),
        input_output_aliases={2: 0},
        compiler_params=pltpu.CompilerParams(dimension_semantics=("arbitrary",),
                                             vmem_limit_bytes=VMEM_LIMIT),
        name="dispatch",
    )(pos, xg, zeros)


def _moe_kernel(e1_ref, e2_ref, nv_ref, xs_ref, wg1, wu1, wd1, wg2, wu2, wd2, ys_ref, *, alpha):
    @pl.when(pl.program_id(0) < nv_ref[0])
    def _():
        x = xs_ref[:, :D_MODEL]
        gates = xs_ref[:, D_MODEL:]
        xb = x.astype(BF16)

        def hidden(wg, wu):
            gt = _dot(xb, wg[...])
            return (gt * _sigmoid(gt) * _dot(xb, wu[...])).astype(BF16)

        h1, h2 = hidden(wg1, wu1), hidden(wg2, wu2)
        ffn = gates[:, 0:1] * _dot(h1, wd1[...]) + gates[:, 1:2] * _dot(h2, wd2[...])
        ys_ref[...] = alpha * x + ffn

    @pl.when(pl.program_id(0) >= nv_ref[0])
    def _():
        ys_ref[...] = jnp.zeros_like(ys_ref)


def _moe(tile_e1, tile_e2, n_valid, xs, wg, wu, wd, layer, alpha):
    n_rows = xs.shape[0]
    n_tiles = n_rows // MOE_TILE
    tile = lambda i, e1, e2, nv: (jnp.maximum(jnp.minimum(i, nv[0] - 1), 0), 0)
    first = lambda i, e1, e2, nv: (layer, e1[i], 0, 0)
    second = lambda i, e1, e2, nv: (layer, e2[i], 0, 0)
    up_spec = lambda f: pl.BlockSpec((None, None, D_MODEL, D_FF), f)
    down_spec = lambda f: pl.BlockSpec((None, None, D_FF, D_MODEL), f)
    return pl.pallas_call(
        functools.partial(_moe_kernel, alpha=alpha),
        out_shape=jax.ShapeDtypeStruct((n_rows, D_MODEL), F32),
        grid_spec=pltpu.PrefetchScalarGridSpec(
            num_scalar_prefetch=3,
            grid=(n_tiles,),
            in_specs=[pl.BlockSpec((MOE_TILE, ROW_W), tile),
                      up_spec(first), up_spec(first), down_spec(first),
                      up_spec(second), up_spec(second), down_spec(second)],
            out_specs=pl.BlockSpec((MOE_TILE, D_MODEL), lambda i, e1, e2, nv: (i, 0))),
        compiler_params=pltpu.CompilerParams(dimension_semantics=("arbitrary",),
                                             vmem_limit_bytes=VMEM_LIMIT),
        name="moe",
    )(tile_e1, tile_e2, n_valid, xs, wg, wu, wd, wg, wu, wd)


def _combine_kernel(pos_ref, ys_ref, p_ref, g_ref, b_ref, wpg_ref, bpg_ref, wple_ref, o_ref, buf, sem):
    tm = o_ref.shape[0]
    i = pl.program_id(0)
    slot = i % 2

    last = pl.num_programs(0) - 1

    def row_copy(tile, r, dst_slot):
        return pltpu.make_async_copy(ys_ref.at[pl.ds(pos_ref[tile * tm + r], 1), :],
                                     buf.at[dst_slot, pl.ds(r, 1), :], sem.at[dst_slot])

    def drain(s):
        pltpu.make_async_copy(ys_ref.at[pl.ds(0, tm), :], buf.at[s], sem.at[s]).wait()

    @pl.when(i == 0)
    def _():
        def issue(g, _):
            for k in range(ROW_UNROLL):
                row_copy(0, g * ROW_UNROLL + k, 0).start()
            return 0
        lax.fori_loop(0, tm // ROW_UNROLL, issue, 0)

    drain(slot)
    xin = buf[slot]
    nxt = jnp.minimum(i + 1, last)
    for r in range(tm):
        row_copy(nxt, r, 1 - slot).start()
    x2 = _layer_norm(xin, g_ref[...], b_ref[...])
    gate = _sigmoid(_dot(x2.astype(BF16), wpg_ref[...]) + bpg_ref[...])
    o_ref[...] = x2 + gate * _dot(p_ref[...].astype(BF16), wple_ref[...])

    @pl.when(i == last)
    def _():
        drain(1 - slot)


def _combine_ple(pos, ys, p3d, g, b, wpg, bpg, wple, layer):
    n = p3d.shape[1]
    tm = TM_ROWS
    row = lambda i, pos: (i, 0)
    const = lambda i, pos: (0, 0)
    layer_const = lambda i, pos: (layer, 0, 0)
    return pl.pallas_call(
        _combine_kernel,
        out_shape=jax.ShapeDtypeStruct((n, D_MODEL), F32),
        grid_spec=pltpu.PrefetchScalarGridSpec(
            num_scalar_prefetch=1,
            grid=(n // tm,),
            in_specs=[pl.BlockSpec(memory_space=pl.ANY),
                      pl.BlockSpec((None, tm, PLE_DIM), lambda i, pos: (layer, i, 0)),
                      pl.BlockSpec((1, D_MODEL), const), pl.BlockSpec((1, D_MODEL), const),
                      pl.BlockSpec((None,) + wpg.shape[1:], layer_const),
                      pl.BlockSpec((1, D_MODEL), const),
                      pl.BlockSpec((None,) + wple.shape[1:], layer_const)],
            out_specs=pl.BlockSpec((tm, D_MODEL), row),
            scratch_shapes=[pltpu.VMEM((2, tm, D_MODEL), F32), pltpu.SemaphoreType.DMA((2,))]),
        compiler_params=pltpu.CompilerParams(dimension_semantics=("arbitrary",),
                                             vmem_limit_bytes=VMEM_LIMIT),
        name="combine_ple",
    )(pos, ys, p3d, g, b, wpg, bpg, wple)


def _tri(n, inclusive_lower):
    j = np.arange(n)[:, None]
    s = np.arange(n)[None, :]
    return jnp.asarray((j >= s) if inclusive_lower else (j <= s), dtype=BF16)


def kernel(x, p, positions, w_in, w_o, sb_norm_g, da_lambda, da_subln_g, ln1_g, ln1_b, ln2_g, ln2_b,
           router_w, router_b, w_gate, w_up, w_down, w_ple, w_ple_gate, b_ple_gate):
    batch, seq, d = x.shape
    depth = w_in.shape[0]
    n = batch * seq
    alpha = (2 * depth) ** 0.25
    max_tiles = n // MOE_TILE + N_CLASSES
    n_rows = max_tiles * MOE_TILE

    inv_freq = ROPE_THETA ** (-jnp.arange(0, ROPE_DIM, 2, dtype=F32) / ROPE_DIM)
    invf_row = jnp.tile(jnp.concatenate([inv_freq, inv_freq, jnp.zeros((HEAD_DIM - ROPE_DIM,), F32)]),
                        LANES // HEAD_DIM)[None, :]
    cos_t, sin_t = _rope_tables(positions.reshape(n, 1), invf_row)

    src = np.array([[4 * g + l for g in range(N_GROUPS)] for l in range(EXPERTS_PER_GROUP)])
    rw_t = jnp.zeros((EXPERTS_PER_GROUP, 8, d), F32).at[:, :N_GROUPS, :].set(router_w.T[src])
    rw_t = rw_t.reshape(ROUTER_ROWS, d)
    rw_hi = rw_t.astype(BF16)
    rw_pad = jnp.concatenate([rw_hi, (rw_t - rw_hi.astype(F32)).astype(BF16)], axis=0)
    rb_col = jnp.zeros((EXPERTS_PER_GROUP, 8), F32).at[:, :N_GROUPS].set(router_b[src]).reshape(ROUTER_ROWS, 1)

    u_sb = _tri(SB_BK, True)
    u_rank = _tri(TM_ROUTE, False)
    class_e1 = jnp.asarray(_CLASS_E1)
    class_e2 = jnp.asarray(_CLASS_E2)

    w_in_b, wg_b, wu_b, wd_b = (w.astype(BF16) for w in (w_in, w_gate, w_up, w_down))
    wpg_b, wple_b = w_ple_gate.astype(BF16), w_ple.astype(BF16)
    p3d = p.reshape(depth, n, PLE_DIM)

    x2d = x.reshape(n, d)
    for i in range(depth):
        lambda_init = 0.8 - 0.6 * math.exp(-0.3 * i)
        sbq, sbk, sbv, daq, dak, dav = _qkv(x2d, w_in_b, cos_t, sin_t, i)
        g_sb = jnp.tile(sb_norm_g[i], LANES // HEAD_DIM)[None, :]
        sb = _sb_attention(sbq, sbk, sbv, u_sb, g_sb, batch, seq)
        da = _da_attention(daq, dak, dav, da_lambda[i], da_subln_g[i][None, :], batch, seq, lambda_init)

        wo = w_o[i].astype(BF16)
        xg, cls, rank, counts = _proj_route(sb, da, x2d, wo[:SB_WIDTH], wo[SB_WIDTH:],
                                            ln1_g[i][None, :], ln1_b[i][None, :], rw_pad, rb_col, u_rank, alpha)

        cnt = counts[:N_CLASSES, 0].astype(jnp.int32)
        tiles_c = (cnt + MOE_TILE - 1) // MOE_TILE
        tile_end = jnp.cumsum(tiles_c)
        row_off = (tile_end - tiles_c) * MOE_TILE
        pos = row_off[cls.reshape(n)] + rank.reshape(n)
        n_valid = tile_end[-1:]
        tile_ids = jnp.minimum(jnp.arange(max_tiles, dtype=jnp.int32), n_valid[0] - 1)
        tile_cls = jnp.minimum(jnp.sum((tile_ids[:, None] >= tile_end[None, :]).astype(jnp.int32), axis=1),
                               N_CLASSES - 1)

        xs = _dispatch(pos, xg, n_rows)
        ys = _moe(class_e1[tile_cls], class_e2[tile_cls], n_valid.astype(jnp.int32), xs,
                  wg_b, wu_b, wd_b, i, alpha)
        x2d = _combine_ple(pos, ys, p3d, ln2_g[i][None, :], ln2_b[i][None, :],
                           wpg_b, b_ple_gate[i][None, :], wple_b, i)
    return x2d.reshape(batch, seq, d)
```

```python
import functools
import math

import numpy as np
import jax
import jax.numpy as jnp
from jax import lax
from jax.experimental import pallas as pl
from jax.experimental.pallas import tpu as pltpu

D_MODEL = 1024
HEAD_DIM = 64
SB_WIDTH = 512
DA_WIDTH = 512
DA_HEADS = 4
ROPE_DIM = 16
ROPE_THETA = 500000.0
N_EXPERTS = 16
N_GROUPS = 4
EXPERTS_PER_GROUP = 4
D_FF = 512
PLE_DIM = 256
LN_EPS = 1e-5

LANES = 128
N_PAIRS = 6
N_CLASSES = N_GROUPS * N_PAIRS
CLASS_ROWS = 32
ROUTER_ROWS = 4 * 8
ROW_UNROLL = 8
GATE_COLS = LANES
ROW_W = D_MODEL + GATE_COLS

MOE_TILE = 256
SB_BQ = 512
SB_BK = 256
DA_BQ = 1024
DA_BK = 512
DA_BK_WIDE = 1024
SB_UNDERFLOW = 104.0
LOG2E = 1.4426950408889634
TM_QKV = 512
TM_ROUTE = 512
TM_DISPATCH = 2048
TM_ROWS = 512
VMEM_LIMIT = 56 * 1024 * 1024

F32 = jnp.float32
BF16 = jnp.bfloat16

_PAIR_LOCAL = [(0, 1), (0, 2), (1, 2), (0, 3), (1, 3), (2, 3)]
_CLASS_E1 = np.array([g * 4 + _PAIR_LOCAL[p][0] for g in range(N_GROUPS) for p in range(N_PAIRS)], np.int32)
_CLASS_E2 = np.array([g * 4 + _PAIR_LOCAL[p][1] for g in range(N_GROUPS) for p in range(N_PAIRS)], np.int32)


def _dot(a, b):
    return jnp.dot(a, b, preferred_element_type=F32)


def _dot_nt(a, b, precision=None):
    return lax.dot_general(a, b, (((1,), (1,)), ((), ())), precision=precision,
                           preferred_element_type=F32)


def _sigmoid(x):
    return 1.0 / (1.0 + jnp.exp(-x))


def _layer_norm(y, g, b):
    mu = jnp.mean(y, axis=-1, keepdims=True)
    yc = y - mu
    var = jnp.mean(yc * yc, axis=-1, keepdims=True)
    return yc * lax.rsqrt(var + LN_EPS) * g + b


def _rope_table_kernel(pos_ref, invf_ref, cos_ref, sin_ref):
    ang = pos_ref[...].astype(F32) * invf_ref[...]
    cos_ref[...] = jnp.cos(ang)
    sin_ref[...] = jnp.sin(ang)


def _rope_tables(pos_col, invf_row):
    n = pos_col.shape[0]
    tm = 1024
    return pl.pallas_call(
        _rope_table_kernel,
        out_shape=(jax.ShapeDtypeStruct((n, LANES), F32), jax.ShapeDtypeStruct((n, LANES), F32)),
        grid=(n // tm,),
        in_specs=[pl.BlockSpec((tm, 1), lambda i: (i, 0)),
                  pl.BlockSpec((1, LANES), lambda i: (0, 0))],
        out_specs=(pl.BlockSpec((tm, LANES), lambda i: (i, 0)),
                   pl.BlockSpec((tm, LANES), lambda i: (i, 0))),
        compiler_params=pltpu.CompilerParams(dimension_semantics=("arbitrary",)),
        name="rope_tables",
    )(pos_col, invf_row)


def _qkv_kernel(x_ref, w_ref, cos_ref, sin_ref, sbq, sbk, sbv, daq, dak, dav):
    xb = x_ref[...].astype(BF16)
    lane = lax.broadcasted_iota(jnp.int32, (1, LANES), 1) % HEAD_DIM
    half = ROPE_DIM // 2
    cosr = jnp.where(lane < ROPE_DIM, cos_ref[...], 1.0)
    sin_lo = jnp.where(lane < half, -sin_ref[...], 0.0)
    sin_hi = jnp.where((lane >= half) & (lane < ROPE_DIM), sin_ref[...], 0.0)
    scale = HEAD_DIM ** -0.5
    outs = (sbq, sbk, sbv, daq, dak, dav)
    for c, o_ref in enumerate(outs):
        hc = _dot(xb, w_ref[:, c * SB_WIDTH:(c + 1) * SB_WIDTH])
        for blk in range(SB_WIDTH // LANES):
            h = hc[:, blk * LANES:(blk + 1) * LANES]
            if c in (3, 4):
                h = (h * cosr + pltpu.roll(h, LANES - half, 1) * sin_lo
                     + pltpu.roll(h, half, 1) * sin_hi)
            if c in (0, 3):
                h = h * scale
            o_ref[:, blk * LANES:(blk + 1) * LANES] = h.astype(BF16)


def _qkv(x2d, w_bf16, cos_t, sin_t, layer):
    n = x2d.shape[0]
    tm = TM_QKV
    row = lambda i: (i, 0)
    out_sd = jax.ShapeDtypeStruct((n, SB_WIDTH), BF16)
    return pl.pallas_call(
        _qkv_kernel,
        out_shape=(out_sd,) * 6,
        grid=(n // tm,),
        in_specs=[pl.BlockSpec((tm, D_MODEL), row),
                  pl.BlockSpec((None,) + w_bf16.shape[1:], lambda i: (layer, 0, 0)),
                  pl.BlockSpec((tm, LANES), row),
                  pl.BlockSpec((tm, LANES), row)],
        out_specs=(pl.BlockSpec((tm, SB_WIDTH), row),) * 6,
        compiler_params=pltpu.CompilerParams(dimension_semantics=("arbitrary",),
                                             vmem_limit_bytes=VMEM_LIMIT),
        name="qkv",
    )(x2d, w_bf16, cos_t, sin_t)


def _stack_halves(q):
    first = lax.broadcasted_iota(jnp.int32, (1, LANES), 1) < HEAD_DIM
    zero = jnp.zeros_like(q)
    return jnp.concatenate([jnp.where(first, q, zero), jnp.where(first, zero, q)], axis=0)


def _local_positions(rows, bq, bk):
    r = lax.broadcasted_iota(jnp.int32, (rows, bk), 0)
    return jnp.where(r >= bq, r - bq, r), lax.broadcasted_iota(jnp.int32, (rows, bk), 1)


def _sb_kernel(q_ref, k_ref, v_ref, u_ref, g_ref, o_ref):
    qi = pl.program_id(2)
    qs = _stack_halves(q_ref[...])
    rows = 2 * SB_BQ
    t_local, s_local = _local_positions(rows, SB_BQ, SB_BK)
    u = u_ref[...]

    def block(qm, kb, carry, acc, mask):
        start = pl.multiple_of(kb * SB_BK, SB_BK)
        ks = k_ref[pl.ds(start, SB_BK), :]
        vs = v_ref[pl.ds(start, SB_BK), :]
        z = _dot_nt(qm, ks)
        sp = jnp.maximum(z, 0.0) + jnp.log(1.0 + jnp.exp2(jnp.abs(z) * (-LOG2E)))
        if mask is not None:
            sp = jnp.where(mask, sp, 0.0)
        tail = _dot(sp.astype(BF16), u)
        a = jnp.exp(z - tail - carry)
        if mask is not None:
            a = jnp.where(mask, a, 0.0)
        acc = acc + _dot(a.astype(BF16), vs)
        carry = carry + jnp.sum(sp, axis=1, keepdims=True)
        return carry, acc

    carry, acc = jnp.zeros((rows, 1), F32), jnp.zeros((rows, LANES), F32)
    per_q = SB_BQ // SB_BK
    for j in reversed(range(per_q)):
        carry, acc = block(qs, qi * per_q + j, carry, acc, s_local + j * SB_BK < t_local)

    def more(st):
        return (st[0] >= 0) & (jnp.min(st[1]) < SB_UNDERFLOW)

    def step(st):
        carry, acc = block(qs, st[0], st[1], st[2], None)
        return st[0] - 1, carry, acc

    _, carry, acc = lax.while_loop(more, step, (qi * per_q - 1, carry, acc))

    first = lax.broadcasted_iota(jnp.int32, (1, LANES), 1) < HEAD_DIM
    o = jnp.where(first, acc[:SB_BQ], acc[SB_BQ:])
    sq = o * o
    ss0 = jnp.sum(jnp.where(first, sq, 0.0), axis=1, keepdims=True)
    ss1 = jnp.sum(jnp.where(first, 0.0, sq), axis=1, keepdims=True)
    ms = jnp.where(first, ss0, ss1) * (1.0 / HEAD_DIM)
    o_ref[...] = (o * lax.rsqrt(ms + LN_EPS) * g_ref[...]).astype(BF16)


def _sb_attention(q, k, v, u_tri, g_row, batch, seq):
    nq = seq // SB_BQ
    n_pairs = SB_WIDTH // LANES
    return pl.pallas_call(
        _sb_kernel,
        out_shape=jax.ShapeDtypeStruct(q.shape, BF16),
        grid=(batch, n_pairs, nq),
        in_specs=[pl.BlockSpec((SB_BQ, LANES), lambda b, h, i: (b * nq + i, h)),
                  pl.BlockSpec((seq, LANES), lambda b, h, i: (b, h)),
                  pl.BlockSpec((seq, LANES), lambda b, h, i: (b, h)),
                  pl.BlockSpec((SB_BK, SB_BK), lambda b, h, i: (0, 0)),
                  pl.BlockSpec((1, LANES), lambda b, h, i: (0, 0))],
        out_specs=pl.BlockSpec((SB_BQ, LANES), lambda b, h, i: (b * nq + i, h)),
        compiler_params=pltpu.CompilerParams(
            dimension_semantics=("arbitrary", "arbitrary", "arbitrary"), vmem_limit_bytes=VMEM_LIMIT),
        name="sb_attn",
    )(q, k, v, u_tri, g_row)


def _da_kernel(q_ref, k_ref, v_ref, lam_ref, g_ref, o_ref, *, lambda_init):
    qi = pl.program_id(2)
    qs = _stack_halves(q_ref[...])
    rows = 2 * DA_BQ
    t_local, s_local = _local_positions(rows, DA_BQ, DA_BK)

    def block(kb, m, l, acc, mask, bk=DA_BK):
        start = pl.multiple_of(kb * bk, bk)
        ks = k_ref[pl.ds(start, bk), :]
        vs = v_ref[pl.ds(start, bk), :]
        s = _dot_nt(qs, ks)
        if mask is not None:
            s = jnp.where(mask, s, -jnp.inf)
        m_blk = jnp.max(s, axis=1, keepdims=True)
        m_new = m_blk if m is None else jnp.maximum(m, m_blk)
        p = jnp.exp(s - m_new)
        pv = _dot(p.astype(BF16), vs)
        ps = jnp.sum(p, axis=1, keepdims=True)
        if m is None:
            return m_new, ps, pv
        alpha = jnp.exp(m - m_new)
        return m_new, alpha * l + ps, alpha * acc + pv

    per_q = DA_BQ // DA_BK
    m = l = acc = None
    for j in range(per_q):
        m, l, acc = block(qi * per_q + j, m, l, acc, s_local + j * DA_BK <= t_local)
    wide = DA_BQ // DA_BK_WIDE
    m, l, acc = lax.fori_loop(0, qi * wide,
                              lambda kb, st: block(kb, st[0], st[1], st[2], None, DA_BK_WIDE), (m, l, acc))
    o_maps = acc / l

    lp = lam_ref[...]
    lam = (jnp.exp(jnp.sum(lp[0:1] * lp[1:2], keepdims=True))
           - jnp.exp(jnp.sum(lp[2:3] * lp[3:4], keepdims=True)) + lambda_init)
    o = o_maps[:DA_BQ] - lam * o_maps[DA_BQ:]
    ms = jnp.mean(o * o, axis=1, keepdims=True)
    o_ref[...] = (o * lax.rsqrt(ms + LN_EPS) * (g_ref[...] * (1.0 - lambda_init))).astype(BF16)


def _da_attention(q, k, v, lam_p, g_row, batch, seq, lambda_init):
    nq = seq // DA_BQ
    return pl.pallas_call(
        functools.partial(_da_kernel, lambda_init=lambda_init),
        out_shape=jax.ShapeDtypeStruct(q.shape, BF16),
        grid=(batch, DA_HEADS, nq),
        in_specs=[pl.BlockSpec((DA_BQ, LANES), lambda b, h, i: (b * nq + i, h)),
                  pl.BlockSpec((seq, LANES), lambda b, h, i: (b, h)),
                  pl.BlockSpec((seq, LANES), lambda b, h, i: (b, h)),
                  pl.BlockSpec(lam_p.shape, lambda b, h, i: (0, 0)),
                  pl.BlockSpec((1, LANES), lambda b, h, i: (0, 0))],
        out_specs=pl.BlockSpec((DA_BQ, LANES), lambda b, h, i: (b * nq + i, h)),
        compiler_params=pltpu.CompilerParams(
            dimension_semantics=("arbitrary", "arbitrary", "arbitrary"), vmem_limit_bytes=VMEM_LIMIT),
        name="da_attn",
    )(q, k, v, lam_p, g_row)


def _route_kernel(sb_ref, da_ref, x_ref, wot_ref, wob_ref, g_ref, b_ref, rw_ref, rb_ref, u_ref,
                  xg_ref, cls_ref, rank_ref, cnt_ref, *, alpha):
    tm = x_ref.shape[0]

    @pl.when(pl.program_id(0) == 0)
    def _():
        cnt_ref[...] = jnp.zeros_like(cnt_ref)

    mix = _dot(sb_ref[...], wot_ref[...]) + _dot(da_ref[...], wob_ref[...])
    x1 = _layer_norm(alpha * x_ref[...] + mix, g_ref[...], b_ref[...])
    xg_ref[:, :D_MODEL] = x1

    x_hi = x1.astype(BF16)
    x_lo = (x1 - x_hi.astype(F32)).astype(BF16)
    by_hi = _dot_nt(rw_ref[...], x_hi)
    logits = (by_hi[0:ROUTER_ROWS] + by_hi[ROUTER_ROWS:]) + _dot_nt(rw_ref[0:ROUTER_ROWS], x_lo)
    scores = _sigmoid(logits)
    sel = scores + rb_ref[...]
    grow = lax.broadcasted_iota(jnp.int32, (8, tm), 0)
    a, b, c, d = sel[0:8], sel[8:16], sel[16:24], sel[24:32]
    hi1, lo1 = jnp.maximum(a, b), jnp.minimum(a, b)
    hi2, lo2 = jnp.maximum(c, d), jnp.minimum(c, d)
    top1 = jnp.maximum(hi1, hi2)
    top2 = jnp.maximum(jnp.minimum(hi1, hi2), jnp.maximum(lo1, lo2))
    gscore = jnp.where(grow < N_GROUPS, top1 + top2, -jnp.inf)
    gmax = jnp.max(gscore, axis=0, keepdims=True)
    grp = jnp.min(jnp.where(gscore == gmax, grow, 8), axis=0, keepdims=True)
    gsel = grow == grp

    one = lambda m: jnp.where(m, 1, 0)
    ranks = (one(b > a) + one(c > a) + one(d > a),
             one(a >= b) + one(c > b) + one(d > b),
             one(a >= c) + one(b >= c) + one(d > c),
             one(a >= d) + one(b >= d) + one(c >= d))
    sc = (scores[0:8], scores[8:16], scores[16:24], scores[24:32])
    chosen = [jnp.where(gsel, one(r < TOP_K), 0) for r in ranks]
    picked = [jnp.sum(ch, axis=0, keepdims=True) for ch in chosen]
    num = [jnp.sum(jnp.where(ch == 1, s, 0.0), axis=0, keepdims=True) for ch, s in zip(chosen, sc)]
    den = num[0] + num[1] + num[2] + num[3]
    gate = [n_ / den for n_ in num]
    g_first = jnp.where(picked[0] == 1, gate[0], jnp.where(picked[1] == 1, gate[1], gate[2]))
    g_second = jnp.where(picked[3] == 1, gate[3], jnp.where(picked[2] == 1, gate[2], gate[1]))
    code = picked[0] + 2 * picked[1] + 4 * picked[2] + 8 * picked[3]
    pair = jnp.where(code == 3, 0, jnp.where(code == 5, 1, jnp.where(code == 6, 2,
           jnp.where(code == 9, 3, jnp.where(code == 10, 4, 5)))))
    cls = grp * N_PAIRS + pair

    crow = lax.broadcasted_iota(jnp.int32, (CLASS_ROWS, tm), 0)
    onehot = jnp.where(crow == cls, 1.0, 0.0)
    prefix = _dot(onehot.astype(BF16), u_ref[...])
    before = cnt_ref[:, 0:1]
    rank = jnp.sum(onehot * (prefix + before), axis=0, keepdims=True) - 1.0
    cnt_ref[...] = cnt_ref[...] + jnp.sum(onehot, axis=1, keepdims=True)
    cls_ref[0] = cls
    rank_ref[0] = rank.astype(jnp.int32)

    trow = lax.broadcasted_iota(jnp.int32, (GATE_COLS, tm), 0)
    gates_t = jnp.where(trow == 0, g_first, jnp.where(trow == 1, g_second, 0.0))
    xg_ref[:, D_MODEL:] = gates_t.T


TOP_K = 2


def _proj_route(sb, da, x2d, wo_top, wo_bot, g, b, rw_t, rb_col, u_tri, alpha):
    n = x2d.shape[0]
    tm = TM_ROUTE
    nt = n // tm
    row = lambda i: (i, 0)
    const = lambda i: (0, 0)
    return pl.pallas_call(
        functools.partial(_route_kernel, alpha=alpha),
        out_shape=(jax.ShapeDtypeStruct((n, ROW_W), F32),
                   jax.ShapeDtypeStruct((nt, 1, tm), jnp.int32),
                   jax.ShapeDtypeStruct((nt, 1, tm), jnp.int32),
                   jax.ShapeDtypeStruct((CLASS_ROWS, LANES), F32)),
        grid=(nt,),
        in_specs=[pl.BlockSpec((tm, SB_WIDTH), row), pl.BlockSpec((tm, DA_WIDTH), row),
                  pl.BlockSpec((tm, D_MODEL), row),
                  pl.BlockSpec(wo_top.shape, const), pl.BlockSpec(wo_bot.shape, const),
                  pl.BlockSpec((1, D_MODEL), const), pl.BlockSpec((1, D_MODEL), const),
                  pl.BlockSpec(rw_t.shape, const), pl.BlockSpec(rb_col.shape, const),
                  pl.BlockSpec((tm, tm), const)],
        out_specs=(pl.BlockSpec((tm, ROW_W), row),
                   pl.BlockSpec((1, 1, tm), lambda i: (i, 0, 0)),
                   pl.BlockSpec((1, 1, tm), lambda i: (i, 0, 0)),
                   pl.BlockSpec((CLASS_ROWS, LANES), const)),
        compiler_params=pltpu.CompilerParams(dimension_semantics=("arbitrary",),
                                             vmem_limit_bytes=VMEM_LIMIT),
        name="proj_route",
    )(sb, da, x2d, wo_top, wo_bot, g, b, rw_t, rb_col, u_tri)


def _dispatch_kernel(pos_ref, xg_ref, zeros_ref, xs_ref, sem):
    del zeros_ref
    tm = xg_ref.shape[0]
    base = pl.program_id(0) * tm

    def issue(g, _):
        for k in range(ROW_UNROLL):
            r = g * ROW_UNROLL + k
            pltpu.make_async_copy(xg_ref.at[pl.ds(r, 1), :],
                                  xs_ref.at[pl.ds(pos_ref[base + r], 1), :], sem).start(priority=k % 2)
        return 0

    lax.fori_loop(0, tm // ROW_UNROLL, issue, 0)
    pltpu.make_async_copy(xg_ref, xs_ref.at[pl.ds(0, tm), :], sem).wait()


def _dispatch(pos, xg, n_rows):
    n = xg.shape[0]
    tm = TM_DISPATCH
    zeros = jnp.zeros((n_rows, ROW_W), F32)
    return pl.pallas_call(
        _dispatch_kernel,
        out_shape=jax.ShapeDtypeStruct((n_rows, ROW_W), F32),
        grid_spec=pltpu.PrefetchScalarGridSpec(
            num_scalar_prefetch=1,
            grid=(n // tm,),
            in_specs=[pl.BlockSpec((tm, ROW_W), lambda i, pos: (i, 0)),
                      pl.BlockSpec(memory_space=pl.ANY)],
            out_specs=pl.BlockSpec(memory_space=pl.ANY),
            scratch_shapes=[pltpu.SemaphoreType.DMA]),
        input_output_aliases={2: 0},
        compiler_params=pltpu.CompilerParams(dimension_semantics=("arbitrary",),
                                             vmem_limit_bytes=VMEM_LIMIT),
        name="dispatch",
    )(pos, xg, zeros)


def _moe_kernel(e1_ref, e2_ref, nv_ref, xs_ref, wg1, wu1, wd1, wg2, wu2, wd2, ys_ref, *, alpha):
    @pl.when(pl.program_id(0) < nv_ref[0])
    def _():
        x = xs_ref[:, :D_MODEL]
        gates = xs_ref[:, D_MODEL:]
        xb = x.astype(BF16)

        def hidden(wg, wu):
            gt = _dot(xb, wg[...])
            return (gt * _sigmoid(gt) * _dot(xb, wu[...])).astype(BF16)

        h1, h2 = hidden(wg1, wu1), hidden(wg2, wu2)
        ffn = gates[:, 0:1] * _dot(h1, wd1[...]) + gates[:, 1:2] * _dot(h2, wd2[...])
        ys_ref[...] = alpha * x + ffn

    @pl.when(pl.program_id(0) >= nv_ref[0])
    def _():
        ys_ref[...] = jnp.zeros_like(ys_ref)


def _moe(tile_e1, tile_e2, n_valid, xs, wg, wu, wd, layer, alpha):
    n_rows = xs.shape[0]
    n_tiles = n_rows // MOE_TILE
    tile = lambda i, e1, e2, nv: (jnp.maximum(jnp.minimum(i, nv[0] - 1), 0), 0)
    first = lambda i, e1, e2, nv: (layer, e1[i], 0, 0)
    second = lambda i, e1, e2, nv: (layer, e2[i], 0, 0)
    up_spec = lambda f: pl.BlockSpec((None, None, D_MODEL, D_FF), f)
    down_spec = lambda f: pl.BlockSpec((None, None, D_FF, D_MODEL), f)
    return pl.pallas_call(
        functools.partial(_moe_kernel, alpha=alpha),
        out_shape=jax.ShapeDtypeStruct((n_rows, D_MODEL), F32),
        grid_spec=pltpu.PrefetchScalarGridSpec(
            num_scalar_prefetch=3,
            grid=(n_tiles,),
            in_specs=[pl.BlockSpec((MOE_TILE, ROW_W), tile),
                      up_spec(first), up_spec(first), down_spec(first),
                      up_spec(second), up_spec(second), down_spec(second)],
            out_specs=pl.BlockSpec((MOE_TILE, D_MODEL), lambda i, e1, e2, nv: (i, 0))),
        compiler_params=pltpu.CompilerParams(dimension_semantics=("arbitrary",),
                                             vmem_limit_bytes=VMEM_LIMIT),
        name="moe",
    )(tile_e1, tile_e2, n_valid, xs, wg, wu, wd, wg, wu, wd)


def _combine_kernel(pos_ref, ys_ref, p_ref, g_ref, b_ref, wpg_ref, bpg_ref, wple_ref, o_ref, buf, sem):
    tm = o_ref.shape[0]
    i = pl.program_id(0)
    slot = i % 2

    last = pl.num_programs(0) - 1

    def row_copy(tile, r, dst_slot):
        return pltpu.make_async_copy(ys_ref.at[pl.ds(pos_ref[tile * tm + r], 1), :],
                                     buf.at[dst_slot, pl.ds(r, 1), :], sem.at[dst_slot])

    def drain(s):
        pltpu.make_async_copy(ys_ref.at[pl.ds(0, tm), :], buf.at[s], sem.at[s]).wait()

    @pl.when(i == 0)
    def _():
        def issue(g, _):
            for k in range(ROW_UNROLL):
                row_copy(0, g * ROW_UNROLL + k, 0).start(priority=k % 2)
            return 0
        lax.fori_loop(0, tm // ROW_UNROLL, issue, 0)

    drain(slot)
    xin = buf[slot]
    nxt = jnp.minimum(i + 1, last)
    for r in range(tm):
        row_copy(nxt, r, 1 - slot).start(priority=r % 2)
    x2 = _layer_norm(xin, g_ref[...], b_ref[...])
    gate = _sigmoid(_dot(x2.astype(BF16), wpg_ref[...]) + bpg_ref[...])
    o_ref[...] = x2 + gate * _dot(p_ref[...].astype(BF16), wple_ref[...])

    @pl.when(i == last)
    def _():
        drain(1 - slot)


def _combine_ple(pos, ys, p3d, g, b, wpg, bpg, wple, layer):
    n = p3d.shape[1]
    tm = TM_ROWS
    row = lambda i, pos: (i, 0)
    const = lambda i, pos: (0, 0)
    layer_const = lambda i, pos: (layer, 0, 0)
    return pl.pallas_call(
        _combine_kernel,
        out_shape=jax.ShapeDtypeStruct((n, D_MODEL), F32),
        grid_spec=pltpu.PrefetchScalarGridSpec(
            num_scalar_prefetch=1,
            grid=(n // tm,),
            in_specs=[pl.BlockSpec(memory_space=pl.ANY),
                      pl.BlockSpec((None, tm, PLE_DIM), lambda i, pos: (layer, i, 0)),
                      pl.BlockSpec((1, D_MODEL), const), pl.BlockSpec((1, D_MODEL), const),
                      pl.BlockSpec((None,) + wpg.shape[1:], layer_const),
                      pl.BlockSpec((1, D_MODEL), const),
                      pl.BlockSpec((None,) + wple.shape[1:], layer_const)],
            out_specs=pl.BlockSpec((tm, D_MODEL), row),
            scratch_shapes=[pltpu.VMEM((2, tm, D_MODEL), F32), pltpu.SemaphoreType.DMA((2,))]),
        compiler_params=pltpu.CompilerParams(dimension_semantics=("arbitrary",),
                                             vmem_limit_bytes=VMEM_LIMIT),
        name="combine_ple",
    )(pos, ys, p3d, g, b, wpg, bpg, wple)


def _tri(n, inclusive_lower):
    j = np.arange(n)[:, None]
    s = np.arange(n)[None, :]
    return jnp.asarray((j >= s) if inclusive_lower else (j <= s), dtype=BF16)


def kernel(x, p, positions, w_in, w_o, sb_norm_g, da_lambda, da_subln_g, ln1_g, ln1_b, ln2_g, ln2_b,
           router_w, router_b, w_gate, w_up, w_down, w_ple, w_ple_gate, b_ple_gate):
    batch, seq, d = x.shape
    depth = w_in.shape[0]
    n = batch * seq
    alpha = (2 * depth) ** 0.25
    max_tiles = n // MOE_TILE + N_CLASSES
    n_rows = max_tiles * MOE_TILE

    inv_freq = ROPE_THETA ** (-jnp.arange(0, ROPE_DIM, 2, dtype=F32) / ROPE_DIM)
    invf_row = jnp.tile(jnp.concatenate([inv_freq, inv_freq, jnp.zeros((HEAD_DIM - ROPE_DIM,), F32)]),
                        LANES // HEAD_DIM)[None, :]
    cos_t, sin_t = _rope_tables(positions.reshape(n, 1), invf_row)

    src = np.array([[4 * g + l for g in range(N_GROUPS)] for l in range(EXPERTS_PER_GROUP)])
    rw_t = jnp.zeros((EXPERTS_PER_GROUP, 8, d), F32).at[:, :N_GROUPS, :].set(router_w.T[src])
    rw_t = rw_t.reshape(ROUTER_ROWS, d)
    rw_hi = rw_t.astype(BF16)
    rw_pad = jnp.concatenate([rw_hi, (rw_t - rw_hi.astype(F32)).astype(BF16)], axis=0)
    rb_col = jnp.zeros((EXPERTS_PER_GROUP, 8), F32).at[:, :N_GROUPS].set(router_b[src]).reshape(ROUTER_ROWS, 1)

    u_sb = _tri(SB_BK, True)
    u_rank = _tri(TM_ROUTE, False)
    class_e1 = jnp.asarray(_CLASS_E1)
    class_e2 = jnp.asarray(_CLASS_E2)

    w_in_b, wg_b, wu_b, wd_b = (w.astype(BF16) for w in (w_in, w_gate, w_up, w_down))
    wpg_b, wple_b = w_ple_gate.astype(BF16), w_ple.astype(BF16)
    p3d = p.reshape(depth, n, PLE_DIM)

    x2d = x.reshape(n, d)
    for i in range(depth):
        lambda_init = 0.8 - 0.6 * math.exp(-0.3 * i)
        sbq, sbk, sbv, daq, dak, dav = _qkv(x2d, w_in_b, cos_t, sin_t, i)
        g_sb = jnp.tile(sb_norm_g[i], LANES // HEAD_DIM)[None, :]
        sb = _sb_attention(sbq, sbk, sbv, u_sb, g_sb, batch, seq)
        da = _da_attention(daq, dak, dav, da_lambda[i], da_subln_g[i][None, :], batch, seq, lambda_init)

        wo = w_o[i].astype(BF16)
        xg, cls, rank, counts = _proj_route(sb, da, x2d, wo[:SB_WIDTH], wo[SB_WIDTH:],
                                            ln1_g[i][None, :], ln1_b[i][None, :], rw_pad, rb_col, u_rank, alpha)

        cnt = counts[:N_CLASSES, 0].astype(jnp.int32)
        tiles_c = (cnt + MOE_TILE - 1) // MOE_TILE
        tile_end = jnp.cumsum(tiles_c)
        row_off = (tile_end - tiles_c) * MOE_TILE
        pos = row_off[cls.reshape(n)] + rank.reshape(n)
        n_valid = tile_end[-1:]
        tile_ids = jnp.minimum(jnp.arange(max_tiles, dtype=jnp.int32), n_valid[0] - 1)
        tile_cls = jnp.minimum(jnp.sum((tile_ids[:, None] >= tile_end[None, :]).astype(jnp.int32), axis=1),
                               N_CLASSES - 1)

        xs = _dispatch(pos, xg, n_rows)
        ys = _moe(class_e1[tile_cls], class_e2[tile_cls], n_valid.astype(jnp.int32), xs,
                  wg_b, wu_b, wd_b, i, alpha)
        x2d = _combine_ple(pos, ys, p3d, ln2_g[i][None, :], ln2_b[i][None, :],
                           wpg_b, b_ple_gate[i][None, :], wple_b, i)
    return x2d.reshape(batch, seq, d)
```

```python
import functools
import math

import numpy as np
import jax
import jax.numpy as jnp
from jax import lax
from jax.experimental import pallas as pl
from jax.experimental.pallas import tpu as pltpu

D_MODEL = 1024
HEAD_DIM = 64
SB_WIDTH = 512
DA_WIDTH = 512
DA_HEADS = 4
ROPE_DIM = 16
ROPE_THETA = 500000.0
N_EXPERTS = 16
N_GROUPS = 4
EXPERTS_PER_GROUP = 4
D_FF = 512
PLE_DIM = 256
LN_EPS = 1e-5

LANES = 128
N_PAIRS = 6
N_CLASSES = N_GROUPS * N_PAIRS
CLASS_ROWS = 32
ROUTER_ROWS = 4 * 8
ROW_UNROLL = 8
GATE_COLS = LANES
ROW_W = D_MODEL + GATE_COLS

MOE_TILE = 256
SB_BQ = 512
SB_BK = 256
DA_BQ = 1024
DA_BK = 512
DA_BK_WIDE = 1024
SB_UNDERFLOW = 104.0
LOG2E = 1.4426950408889634
TM_QKV = 512
TM_ROUTE = 512
TM_DISPATCH = 2048
TM_ROWS = 512
VMEM_LIMIT = 56 * 1024 * 1024

F32 = jnp.float32
BF16 = jnp.bfloat16

_PAIR_LOCAL = [(0, 1), (0, 2), (1, 2), (0, 3), (1, 3), (2, 3)]
_CLASS_E1 = np.array([g * 4 + _PAIR_LOCAL[p][0] for g in range(N_GROUPS) for p in range(N_PAIRS)], np.int32)
_CLASS_E2 = np.array([g * 4 + _PAIR_LOCAL[p][1] for g in range(N_GROUPS) for p in range(N_PAIRS)], np.int32)


def _dot(a, b):
    return jnp.dot(a, b, preferred_element_type=F32)


def _dot_nt(a, b, precision=None):
    return lax.dot_general(a, b, (((1,), (1,)), ((), ())), precision=precision,
                           preferred_element_type=F32)


def _sigmoid(x):
    return 1.0 / (1.0 + jnp.exp(-x))


def _layer_norm(y, g, b):
    mu = jnp.mean(y, axis=-1, keepdims=True)
    yc = y - mu
    var = jnp.mean(yc * yc, axis=-1, keepdims=True)
    return yc * lax.rsqrt(var + LN_EPS) * g + b


def _rope_table_kernel(pos_ref, invf_ref, cos_ref, sin_ref):
    ang = pos_ref[...].astype(F32) * invf_ref[...]
    cos_ref[...] = jnp.cos(ang)
    sin_ref[...] = jnp.sin(ang)


def _rope_tables(pos_col, invf_row):
    n = pos_col.shape[0]
    tm = 1024
    return pl.pallas_call(
        _rope_table_kernel,
        out_shape=(jax.ShapeDtypeStruct((n, LANES), F32), jax.ShapeDtypeStruct((n, LANES), F32)),
        grid=(n // tm,),
        in_specs=[pl.BlockSpec((tm, 1), lambda i: (i, 0)),
                  pl.BlockSpec((1, LANES), lambda i: (0, 0))],
        out_specs=(pl.BlockSpec((tm, LANES), lambda i: (i, 0)),
                   pl.BlockSpec((tm, LANES), lambda i: (i, 0))),
        compiler_params=pltpu.CompilerParams(dimension_semantics=("arbitrary",)),
        name="rope_tables",
    )(pos_col, invf_row)


def _qkv_kernel(x_ref, w_ref, cos_ref, sin_ref, sbq, sbk, sbv, daq, dak, dav):
    xb = x_ref[...].astype(BF16)
    lane = lax.broadcasted_iota(jnp.int32, (1, LANES), 1) % HEAD_DIM
    half = ROPE_DIM // 2
    cosr = jnp.where(lane < ROPE_DIM, cos_ref[...], 1.0)
    sin_lo = jnp.where(lane < half, -sin_ref[...], 0.0)
    sin_hi = jnp.where((lane >= half) & (lane < ROPE_DIM), sin_ref[...], 0.0)
    scale = HEAD_DIM ** -0.5
    outs = (sbq, sbk, sbv, daq, dak, dav)
    for c, o_ref in enumerate(outs):
        hc = _dot(xb, w_ref[:, c * SB_WIDTH:(c + 1) * SB_WIDTH])
        for blk in range(SB_WIDTH // LANES):
            h = hc[:, blk * LANES:(blk + 1) * LANES]
            if c in (3, 4):
                h = (h * cosr + pltpu.roll(h, LANES - half, 1) * sin_lo
                     + pltpu.roll(h, half, 1) * sin_hi)
            if c in (0, 3):
                h = h * scale
            o_ref[:, blk * LANES:(blk + 1) * LANES] = h.astype(BF16)


def _qkv(x2d, w_bf16, cos_t, sin_t, layer):
    n = x2d.shape[0]
    tm = TM_QKV
    row = lambda i: (i, 0)
    out_sd = jax.ShapeDtypeStruct((n, SB_WIDTH), BF16)
    return pl.pallas_call(
        _qkv_kernel,
        out_shape=(out_sd,) * 6,
        grid=(n // tm,),
        in_specs=[pl.BlockSpec((tm, D_MODEL), row),
                  pl.BlockSpec((None,) + w_bf16.shape[1:], lambda i: (layer, 0, 0)),
                  pl.BlockSpec((tm, LANES), row),
                  pl.BlockSpec((tm, LANES), row)],
        out_specs=(pl.BlockSpec((tm, SB_WIDTH), row),) * 6,
        compiler_params=pltpu.CompilerParams(dimension_semantics=("arbitrary",),
                                             vmem_limit_bytes=VMEM_LIMIT),
        name="qkv",
    )(x2d, w_bf16, cos_t, sin_t)


def _stack_halves(q):
    first = lax.broadcasted_iota(jnp.int32, (1, LANES), 1) < HEAD_DIM
    zero = jnp.zeros_like(q)
    return jnp.concatenate([jnp.where(first, q, zero), jnp.where(first, zero, q)], axis=0)


def _local_positions(rows, bq, bk):
    r = lax.broadcasted_iota(jnp.int32, (rows, bk), 0)
    return jnp.where(r >= bq, r - bq, r), lax.broadcasted_iota(jnp.int32, (rows, bk), 1)


def _sb_kernel(q_ref, k_ref, v_ref, u_ref, g_ref, o_ref):
    qi = pl.program_id(2)
    qs = _stack_halves(q_ref[...])
    rows = 2 * SB_BQ
    t_local, s_local = _local_positions(rows, SB_BQ, SB_BK)
    u = u_ref[...]

    def block(qm, kb, carry, acc, mask):
        start = pl.multiple_of(kb * SB_BK, SB_BK)
        ks = k_ref[pl.ds(start, SB_BK), :]
        vs = v_ref[pl.ds(start, SB_BK), :]
        z = _dot_nt(qm, ks)
        sp = jnp.maximum(z, 0.0) + jnp.log(1.0 + jnp.exp2(jnp.abs(z) * (-LOG2E)))
        if mask is not None:
            sp = jnp.where(mask, sp, 0.0)
        tail = _dot(sp.astype(BF16), u)
        a = jnp.exp(z - tail - carry)
        if mask is not None:
            a = jnp.where(mask, a, 0.0)
        acc = acc + _dot(a.astype(BF16), vs)
        carry = carry + jnp.sum(sp, axis=1, keepdims=True)
        return carry, acc

    carry, acc = jnp.zeros((rows, 1), F32), jnp.zeros((rows, LANES), F32)
    per_q = SB_BQ // SB_BK
    for j in reversed(range(per_q)):
        carry, acc = block(qs, qi * per_q + j, carry, acc, s_local + j * SB_BK < t_local)

    def more(st):
        return (st[0] >= 0) & (jnp.min(st[1]) < SB_UNDERFLOW)

    def step(st):
        carry, acc = block(qs, st[0], st[1], st[2], None)
        return st[0] - 1, carry, acc

    _, carry, acc = lax.while_loop(more, step, (qi * per_q - 1, carry, acc))

    first = lax.broadcasted_iota(jnp.int32, (1, LANES), 1) < HEAD_DIM
    o = jnp.where(first, acc[:SB_BQ], acc[SB_BQ:])
    sq = o * o
    ss0 = jnp.sum(jnp.where(first, sq, 0.0), axis=1, keepdims=True)
    ss1 = jnp.sum(jnp.where(first, 0.0, sq), axis=1, keepdims=True)
    ms = jnp.where(first, ss0, ss1) * (1.0 / HEAD_DIM)
    o_ref[...] = (o * lax.rsqrt(ms + LN_EPS) * g_ref[...]).astype(BF16)


def _sb_attention(q, k, v, u_tri, g_row, batch, seq):
    nq = seq // SB_BQ
    n_pairs = SB_WIDTH // LANES
    return pl.pallas_call(
        _sb_kernel,
        out_shape=jax.ShapeDtypeStruct(q.shape, BF16),
        grid=(batch, n_pairs, nq),
        in_specs=[pl.BlockSpec((SB_BQ, LANES), lambda b, h, i: (b * nq + i, h)),
                  pl.BlockSpec((seq, LANES), lambda b, h, i: (b, h)),
                  pl.BlockSpec((seq, LANES), lambda b, h, i: (b, h)),
                  pl.BlockSpec((SB_BK, SB_BK), lambda b, h, i: (0, 0)),
                  pl.BlockSpec((1, LANES), lambda b, h, i: (0, 0))],
        out_specs=pl.BlockSpec((SB_BQ, LANES), lambda b, h, i: (b * nq + i, h)),
        compiler_params=pltpu.CompilerParams(
            dimension_semantics=("arbitrary", "arbitrary", "arbitrary"), vmem_limit_bytes=VMEM_LIMIT),
        name="sb_attn",
    )(q, k, v, u_tri, g_row)


def _da_kernel(q_ref, k_ref, v_ref, lam_ref, g_ref, o_ref, *, lambda_init):
    qi = pl.program_id(2)
    qs = _stack_halves(q_ref[...])
    rows = 2 * DA_BQ
    t_local, s_local = _local_positions(rows, DA_BQ, DA_BK)

    def block(kb, m, l, acc, mask, bk=DA_BK):
        start = pl.multiple_of(kb * bk, bk)
        ks = k_ref[pl.ds(start, bk), :]
        vs = v_ref[pl.ds(start, bk), :]
        s = _dot_nt(qs, ks)
        if mask is not None:
            s = jnp.where(mask, s, -jnp.inf)
        m_blk = jnp.max(s, axis=1, keepdims=True)
        m_new = m_blk if m is None else jnp.maximum(m, m_blk)
        p = jnp.exp(s - m_new)
        pv = _dot(p.astype(BF16), vs)
        ps = jnp.sum(p, axis=1, keepdims=True)
        if m is None:
            return m_new, ps, pv
        alpha = jnp.exp(m - m_new)
        return m_new, alpha * l + ps, alpha * acc + pv

    per_q = DA_BQ // DA_BK
    m = l = acc = None
    for j in range(per_q):
        m, l, acc = block(qi * per_q + j, m, l, acc, s_local + j * DA_BK <= t_local)
    wide = DA_BQ // DA_BK_WIDE
    m, l, acc = lax.fori_loop(0, qi * wide,
                              lambda kb, st: block(kb, st[0], st[1], st[2], None, DA_BK_WIDE), (m, l, acc))
    o_maps = acc / l

    lp = lam_ref[...]
    lam = (jnp.exp(jnp.sum(lp[0:1] * lp[1:2], keepdims=True))
           - jnp.exp(jnp.sum(lp[2:3] * lp[3:4], keepdims=True)) + lambda_init)
    o = o_maps[:DA_BQ] - lam * o_maps[DA_BQ:]
    ms = jnp.mean(o * o, axis=1, keepdims=True)
    o_ref[...] = (o * lax.rsqrt(ms + LN_EPS) * (g_ref[...] * (1.0 - lambda_init))).astype(BF16)


def _da_attention(q, k, v, lam_p, g_row, batch, seq, lambda_init):
    nq = seq // DA_BQ
    return pl.pallas_call(
        functools.partial(_da_kernel, lambda_init=lambda_init),
        out_shape=jax.ShapeDtypeStruct(q.shape, BF16),
        grid=(batch, DA_HEADS, nq),
        in_specs=[pl.BlockSpec((DA_BQ, LANES), lambda b, h, i: (b * nq + i, h)),
                  pl.BlockSpec((seq, LANES), lambda b, h, i: (b, h)),
                  pl.BlockSpec((seq, LANES), lambda b, h, i: (b, h)),
                  pl.BlockSpec(lam_p.shape, lambda b, h, i: (0, 0)),
                  pl.BlockSpec((1, LANES), lambda b, h, i: (0, 0))],
        out_specs=pl.BlockSpec((DA_BQ, LANES), lambda b, h, i: (b * nq + i, h)),
        compiler_params=pltpu.CompilerParams(
            dimension_semantics=("arbitrary", "arbitrary", "arbitrary"), vmem_limit_bytes=VMEM_LIMIT),
        name="da_attn",
    )(q, k, v, lam_p, g_row)


def _route_kernel(sb_ref, da_ref, x_ref, wot_ref, wob_ref, g_ref, b_ref, rw_ref, rb_ref, u_ref,
                  xg_ref, cls_ref, rank_ref, cnt_ref, *, alpha):
    tm = x_ref.shape[0]

    @pl.when(pl.program_id(0) == 0)
    def _():
        cnt_ref[...] = jnp.zeros_like(cnt_ref)

    mix = _dot(sb_ref[...], wot_ref[...]) + _dot(da_ref[...], wob_ref[...])
    x1 = _layer_norm(alpha * x_ref[...] + mix, g_ref[...], b_ref[...])
    xg_ref[:, :D_MODEL] = x1

    x_hi = x1.astype(BF16)
    x_lo = (x1 - x_hi.astype(F32)).astype(BF16)
    by_hi = _dot_nt(rw_ref[...], x_hi)
    logits = (by_hi[0:ROUTER_ROWS] + by_hi[ROUTER_ROWS:]) + _dot_nt(rw_ref[0:ROUTER_ROWS], x_lo)
    scores = _sigmoid(logits)
    sel = scores + rb_ref[...]
    grow = lax.broadcasted_iota(jnp.int32, (8, tm), 0)
    a, b, c, d = sel[0:8], sel[8:16], sel[16:24], sel[24:32]
    hi1, lo1 = jnp.maximum(a, b), jnp.minimum(a, b)
    hi2, lo2 = jnp.maximum(c, d), jnp.minimum(c, d)
    top1 = jnp.maximum(hi1, hi2)
    top2 = jnp.maximum(jnp.minimum(hi1, hi2), jnp.maximum(lo1, lo2))
    gscore = jnp.where(grow < N_GROUPS, top1 + top2, -jnp.inf)
    gmax = jnp.max(gscore, axis=0, keepdims=True)
    grp = jnp.min(jnp.where(gscore == gmax, grow, 8), axis=0, keepdims=True)
    gsel = grow == grp

    one = lambda m: jnp.where(m, 1, 0)
    ranks = (one(b > a) + one(c > a) + one(d > a),
             one(a >= b) + one(c > b) + one(d > b),
             one(a >= c) + one(b >= c) + one(d > c),
             one(a >= d) + one(b >= d) + one(c >= d))
    sc = (scores[0:8], scores[8:16], scores[16:24], scores[24:32])
    chosen = [jnp.where(gsel, one(r < TOP_K), 0) for r in ranks]
    picked = [jnp.sum(ch, axis=0, keepdims=True) for ch in chosen]
    num = [jnp.sum(jnp.where(ch == 1, s, 0.0), axis=0, keepdims=True) for ch, s in zip(chosen, sc)]
    den = num[0] + num[1] + num[2] + num[3]
    gate = [n_ / den for n_ in num]
    g_first = jnp.where(picked[0] == 1, gate[0], jnp.where(picked[1] == 1, gate[1], gate[2]))
    g_second = jnp.where(picked[3] == 1, gate[3], jnp.where(picked[2] == 1, gate[2], gate[1]))
    code = picked[0] + 2 * picked[1] + 4 * picked[2] + 8 * picked[3]
    pair = jnp.where(code == 3, 0, jnp.where(code == 5, 1, jnp.where(code == 6, 2,
           jnp.where(code == 9, 3, jnp.where(code == 10, 4, 5)))))
    cls = grp * N_PAIRS + pair

    crow = lax.broadcasted_iota(jnp.int32, (CLASS_ROWS, tm), 0)
    onehot = jnp.where(crow == cls, 1.0, 0.0)
    prefix = _dot(onehot.astype(BF16), u_ref[...])
    before = cnt_ref[:, 0:1]
    rank = jnp.sum(onehot * (prefix + before), axis=0, keepdims=True) - 1.0
    cnt_ref[...] = cnt_ref[...] + jnp.sum(onehot, axis=1, keepdims=True)
    cls_ref[0] = cls
    rank_ref[0] = rank.astype(jnp.int32)

    trow = lax.broadcasted_iota(jnp.int32, (GATE_COLS, tm), 0)
    gates_t = jnp.where(trow == 0, g_first, jnp.where(trow == 1, g_second, 0.0))
    xg_ref[:, D_MODEL:] = gates_t.T


TOP_K = 2


def _proj_route(sb, da, x2d, wo_top, wo_bot, g, b, rw_t, rb_col, u_tri, alpha):
    n = x2d.shape[0]
    tm = TM_ROUTE
    nt = n // tm
    row = lambda i: (i, 0)
    const = lambda i: (0, 0)
    return pl.pallas_call(
        functools.partial(_route_kernel, alpha=alpha),
        out_shape=(jax.ShapeDtypeStruct((n, ROW_W), F32),
                   jax.ShapeDtypeStruct((nt, 1, tm), jnp.int32),
                   jax.ShapeDtypeStruct((nt, 1, tm), jnp.int32),
                   jax.ShapeDtypeStruct((CLASS_ROWS, LANES), F32)),
        grid=(nt,),
        in_specs=[pl.BlockSpec((tm, SB_WIDTH), row), pl.BlockSpec((tm, DA_WIDTH), row),
                  pl.BlockSpec((tm, D_MODEL), row),
                  pl.BlockSpec(wo_top.shape, const), pl.BlockSpec(wo_bot.shape, const),
                  pl.BlockSpec((1, D_MODEL), const), pl.BlockSpec((1, D_MODEL), const),
                  pl.BlockSpec(rw_t.shape, const), pl.BlockSpec(rb_col.shape, const),
                  pl.BlockSpec((tm, tm), const)],
        out_specs=(pl.BlockSpec((tm, ROW_W), row),
                   pl.BlockSpec((1, 1, tm), lambda i: (i, 0, 0)),
                   pl.BlockSpec((1, 1, tm), lambda i: (i, 0, 0)),
                   pl.BlockSpec((CLASS_ROWS, LANES), const)),
        compiler_params=pltpu.CompilerParams(dimension_semantics=("arbitrary",),
                                             vmem_limit_bytes=VMEM_LIMIT),
        name="proj_route",
    )(sb, da, x2d, wo_top, wo_bot, g, b, rw_t, rb_col, u_tri)


def _dispatch_kernel(pos_ref, xg_ref, zeros_ref, xs_ref, sem):
    del zeros_ref
    groups = xg_ref.shape[0]
    tm = groups * ROW_UNROLL
    base = pl.program_id(0) * tm

    def issue(g, _):
        for k in range(ROW_UNROLL):
            pltpu.make_async_copy(xg_ref.at[g, pl.ds(k, 1), :],
                                  xs_ref.at[pl.ds(pos_ref[base + g * ROW_UNROLL + k], 1), :],
                                  sem).start(priority=k % 2)
        return 0

    lax.fori_loop(0, groups, issue, 0)
    whole = xs_ref.at[pl.ds(0, tm), :]
    pltpu.make_async_copy(whole, whole, sem).wait()


def _dispatch(pos, xg, n_rows):
    n = xg.shape[0]
    tm = TM_DISPATCH
    assert ROW_UNROLL == 8
    xg = xg.reshape(n // ROW_UNROLL, ROW_UNROLL, ROW_W)
    zeros = jnp.zeros((n_rows, ROW_W), F32)
    return pl.pallas_call(
        _dispatch_kernel,
        out_shape=jax.ShapeDtypeStruct((n_rows, ROW_W), F32),
        grid_spec=pltpu.PrefetchScalarGridSpec(
            num_scalar_prefetch=1,
            grid=(n // tm,),
            in_specs=[pl.BlockSpec((tm // ROW_UNROLL, ROW_UNROLL, ROW_W), lambda i, pos: (i, 0, 0)),
                      pl.BlockSpec(memory_space=pl.ANY)],
            out_specs=pl.BlockSpec(memory_space=pl.ANY),
            scratch_shapes=[pltpu.SemaphoreType.DMA]),
        input_output_aliases={2: 0},
        compiler_params=pltpu.CompilerParams(dimension_semantics=("arbitrary",),
                                             vmem_limit_bytes=VMEM_LIMIT),
        name="dispatch",
    )(pos, xg, zeros)


def _moe_kernel(e1_ref, e2_ref, nv_ref, xs_ref, wg1, wu1, wd1, wg2, wu2, wd2, ys_ref, *, alpha):
    @pl.when(pl.program_id(0) < nv_ref[0])
    def _():
        x = xs_ref[:, :D_MODEL]
        gates = xs_ref[:, D_MODEL:]
        xb = x.astype(BF16)

        def hidden(wg, wu):
            gt = _dot(xb, wg[...])
            return (gt * _sigmoid(gt) * _dot(xb, wu[...])).astype(BF16)

        h1, h2 = hidden(wg1, wu1), hidden(wg2, wu2)
        ffn = gates[:, 0:1] * _dot(h1, wd1[...]) + gates[:, 1:2] * _dot(h2, wd2[...])
        ys_ref[...] = alpha * x + ffn

    @pl.when(pl.program_id(0) >= nv_ref[0])
    def _():
        ys_ref[...] = jnp.zeros_like(ys_ref)


def _moe(tile_e1, tile_e2, n_valid, xs, wg, wu, wd, layer, alpha):
    n_rows = xs.shape[0]
    n_tiles = n_rows // MOE_TILE
    tile = lambda i, e1, e2, nv: (jnp.maximum(jnp.minimum(i, nv[0] - 1), 0), 0)
    first = lambda i, e1, e2, nv: (layer, e1[i], 0, 0)
    second = lambda i, e1, e2, nv: (layer, e2[i], 0, 0)
    up_spec = lambda f: pl.BlockSpec((None, None, D_MODEL, D_FF), f)
    down_spec = lambda f: pl.BlockSpec((None, None, D_FF, D_MODEL), f)
    return pl.pallas_call(
        functools.partial(_moe_kernel, alpha=alpha),
        out_shape=jax.ShapeDtypeStruct((n_rows, D_MODEL), F32),
        grid_spec=pltpu.PrefetchScalarGridSpec(
            num_scalar_prefetch=3,
            grid=(n_tiles,),
            in_specs=[pl.BlockSpec((MOE_TILE, ROW_W), tile),
                      up_spec(first), up_spec(first), down_spec(first),
                      up_spec(second), up_spec(second), down_spec(second)],
            out_specs=pl.BlockSpec((MOE_TILE, D_MODEL), lambda i, e1, e2, nv: (i, 0))),
        compiler_params=pltpu.CompilerParams(dimension_semantics=("arbitrary",),
                                             vmem_limit_bytes=VMEM_LIMIT),
        name="moe",
    )(tile_e1, tile_e2, n_valid, xs, wg, wu, wd, wg, wu, wd)


def _combine_kernel(pos_ref, ys_ref, p_ref, g_ref, b_ref, wpg_ref, bpg_ref, wple_ref, o_ref, buf, sem):
    tm = o_ref.shape[0]
    i = pl.program_id(0)
    slot = i % 2

    last = pl.num_programs(0) - 1

    def row_copy(tile, r, dst_slot):
        return pltpu.make_async_copy(ys_ref.at[pl.ds(pos_ref[tile * tm + r], 1), :],
                                     buf.at[dst_slot, pl.ds(r, 1), :], sem.at[dst_slot])

    def drain(s):
        pltpu.make_async_copy(ys_ref.at[pl.ds(0, tm), :], buf.at[s], sem.at[s]).wait()

    @pl.when(i == 0)
    def _():
        def issue(g, _):
            for k in range(ROW_UNROLL):
                row_copy(0, g * ROW_UNROLL + k, 0).start(priority=k % 2)
            return 0
        lax.fori_loop(0, tm // ROW_UNROLL, issue, 0)

    drain(slot)
    xin = buf[slot]
    nxt = jnp.minimum(i + 1, last)
    for r in range(tm):
        row_copy(nxt, r, 1 - slot).start(priority=r % 2)
    x2 = _layer_norm(xin, g_ref[...], b_ref[...])
    gate = _sigmoid(_dot(x2.astype(BF16), wpg_ref[...]) + bpg_ref[...])
    o_ref[...] = x2 + gate * _dot(p_ref[...].astype(BF16), wple_ref[...])

    @pl.when(i == last)
    def _():
        drain(1 - slot)


def _combine_ple(pos, ys, p3d, g, b, wpg, bpg, wple, layer):
    n = p3d.shape[1]
    tm = TM_ROWS
    row = lambda i, pos: (i, 0)
    const = lambda i, pos: (0, 0)
    layer_const = lambda i, pos: (layer, 0, 0)
    return pl.pallas_call(
        _combine_kernel,
        out_shape=jax.ShapeDtypeStruct((n, D_MODEL), F32),
        grid_spec=pltpu.PrefetchScalarGridSpec(
            num_scalar_prefetch=1,
            grid=(n // tm,),
            in_specs=[pl.BlockSpec(memory_space=pl.ANY),
                      pl.BlockSpec((None, tm, PLE_DIM), lambda i, pos: (layer, i, 0)),
                      pl.BlockSpec((1, D_MODEL), const), pl.BlockSpec((1, D_MODEL), const),
                      pl.BlockSpec((None,) + wpg.shape[1:], layer_const),
                      pl.BlockSpec((1, D_MODEL), const),
                      pl.BlockSpec((None,) + wple.shape[1:], layer_const)],
            out_specs=pl.BlockSpec((tm, D_MODEL), row),
            scratch_shapes=[pltpu.VMEM((2, tm, D_MODEL), F32), pltpu.SemaphoreType.DMA((2,))]),
        compiler_params=pltpu.CompilerParams(dimension_semantics=("arbitrary",),
                                             vmem_limit_bytes=VMEM_LIMIT),
        name="combine_ple",
    )(pos, ys, p3d, g, b, wpg, bpg, wple)


def _tri(n, inclusive_lower):
    j = np.arange(n)[:, None]
    s = np.arange(n)[None, :]
    return jnp.asarray((j >= s) if inclusive_lower else (j <= s), dtype=BF16)


def kernel(x, p, positions, w_in, w_o, sb_norm_g, da_lambda, da_subln_g, ln1_g, ln1_b, ln2_g, ln2_b,
           router_w, router_b, w_gate, w_up, w_down, w_ple, w_ple_gate, b_ple_gate):
    batch, seq, d = x.shape
    depth = w_in.shape[0]
    n = batch * seq
    alpha = (2 * depth) ** 0.25
    max_tiles = n // MOE_TILE + N_CLASSES
    n_rows = max_tiles * MOE_TILE

    inv_freq = ROPE_THETA ** (-jnp.arange(0, ROPE_DIM, 2, dtype=F32) / ROPE_DIM)
    invf_row = jnp.tile(jnp.concatenate([inv_freq, inv_freq, jnp.zeros((HEAD_DIM - ROPE_DIM,), F32)]),
                        LANES // HEAD_DIM)[None, :]
    cos_t, sin_t = _rope_tables(positions.reshape(n, 1), invf_row)

    src = np.array([[4 * g + l for g in range(N_GROUPS)] for l in range(EXPERTS_PER_GROUP)])
    rw_t = jnp.zeros((EXPERTS_PER_GROUP, 8, d), F32).at[:, :N_GROUPS, :].set(router_w.T[src])
    rw_t = rw_t.reshape(ROUTER_ROWS, d)
    rw_hi = rw_t.astype(BF16)
    rw_pad = jnp.concatenate([rw_hi, (rw_t - rw_hi.astype(F32)).astype(BF16)], axis=0)
    rb_col = jnp.zeros((EXPERTS_PER_GROUP, 8), F32).at[:, :N_GROUPS].set(router_b[src]).reshape(ROUTER_ROWS, 1)

    u_sb = _tri(SB_BK, True)
    u_rank = _tri(TM_ROUTE, False)
    class_e1 = jnp.asarray(_CLASS_E1)
    class_e2 = jnp.asarray(_CLASS_E2)

    w_in_b, wg_b, wu_b, wd_b = (w.astype(BF16) for w in (w_in, w_gate, w_up, w_down))
    wpg_b, wple_b = w_ple_gate.astype(BF16), w_ple.astype(BF16)
    p3d = p.reshape(depth, n, PLE_DIM)

    x2d = x.reshape(n, d)
    for i in range(depth):
        lambda_init = 0.8 - 0.6 * math.exp(-0.3 * i)
        sbq, sbk, sbv, daq, dak, dav = _qkv(x2d, w_in_b, cos_t, sin_t, i)
        g_sb = jnp.tile(sb_norm_g[i], LANES // HEAD_DIM)[None, :]
        sb = _sb_attention(sbq, sbk, sbv, u_sb, g_sb, batch, seq)
        da = _da_attention(daq, dak, dav, da_lambda[i], da_subln_g[i][None, :], batch, seq, lambda_init)

        wo = w_o[i].astype(BF16)
        xg, cls, rank, counts = _proj_route(sb, da, x2d, wo[:SB_WIDTH], wo[SB_WIDTH:],
                                            ln1_g[i][None, :], ln1_b[i][None, :], rw_pad, rb_col, u_rank, alpha)

        cnt = counts[:N_CLASSES, 0].astype(jnp.int32)
        tiles_c = (cnt + MOE_TILE - 1) // MOE_TILE
        tile_end = jnp.cumsum(tiles_c)
        row_off = (tile_end - tiles_c) * MOE_TILE
        pos = row_off[cls.reshape(n)] + rank.reshape(n)
        n_valid = tile_end[-1:]
        tile_ids = jnp.minimum(jnp.arange(max_tiles, dtype=jnp.int32), n_valid[0] - 1)
        tile_cls = jnp.minimum(jnp.sum((tile_ids[:, None] >= tile_end[None, :]).astype(jnp.int32), axis=1),
                               N_CLASSES - 1)

        xs = _dispatch(pos, xg, n_rows)
        ys = _moe(class_e1[tile_cls], class_e2[tile_cls], n_valid.astype(jnp.int32), xs,
                  wg_b, wu_b, wd_b, i, alpha)
        x2d = _combine_ple(pos, ys, p3d, ln2_g[i][None, :], ln2_b[i][None, :],
                           wpg_b, b_ple_gate[i][None, :], wple_b, i)
    return x2d.reshape(batch, seq, d)
```

```python
import functools
import math

import numpy as np
import jax
import jax.numpy as jnp
from jax import lax
from jax.experimental import pallas as pl
from jax.experimental.pallas import tpu as pltpu

D_MODEL = 1024
HEAD_DIM = 64
SB_WIDTH = 512
DA_WIDTH = 512
DA_HEADS = 4
ROPE_DIM = 16
ROPE_THETA = 500000.0
N_EXPERTS = 16
N_GROUPS = 4
EXPERTS_PER_GROUP = 4
D_FF = 512
PLE_DIM = 256
LN_EPS = 1e-5

LANES = 128
N_PAIRS = 6
N_CLASSES = N_GROUPS * N_PAIRS
CLASS_ROWS = 32
ROUTER_ROWS = 4 * 8
ROW_UNROLL = 8
GATE_COLS = LANES
ROW_W = D_MODEL + GATE_COLS

MOE_TILE = 256
SB_BQ = 512
SB_BK = 256
DA_BQ = 1024
DA_BK = 512
DA_BK_WIDE = 1024
SB_UNDERFLOW = 104.0
LOG2E = 1.4426950408889634
TM_QKV = 512
TM_ROUTE = 512
TM_DISPATCH = 4096
TM_ROWS = 512
VMEM_LIMIT = 56 * 1024 * 1024

F32 = jnp.float32
BF16 = jnp.bfloat16

_PAIR_LOCAL = [(0, 1), (0, 2), (1, 2), (0, 3), (1, 3), (2, 3)]
_CLASS_E1 = np.array([g * 4 + _PAIR_LOCAL[p][0] for g in range(N_GROUPS) for p in range(N_PAIRS)], np.int32)
_CLASS_E2 = np.array([g * 4 + _PAIR_LOCAL[p][1] for g in range(N_GROUPS) for p in range(N_PAIRS)], np.int32)


def _dot(a, b):
    return jnp.dot(a, b, preferred_element_type=F32)


def _dot_nt(a, b, precision=None):
    return lax.dot_general(a, b, (((1,), (1,)), ((), ())), precision=precision,
                           preferred_element_type=F32)


def _sigmoid(x):
    return 1.0 / (1.0 + jnp.exp(-x))


def _layer_norm(y, g, b):
    mu = jnp.mean(y, axis=-1, keepdims=True)
    yc = y - mu
    var = jnp.mean(yc * yc, axis=-1, keepdims=True)
    return yc * lax.rsqrt(var + LN_EPS) * g + b


def _rope_table_kernel(pos_ref, invf_ref, cos_ref, sin_ref):
    ang = pos_ref[...].astype(F32) * invf_ref[...]
    cos_ref[...] = jnp.cos(ang)
    sin_ref[...] = jnp.sin(ang)


def _rope_tables(pos_col, invf_row):
    n = pos_col.shape[0]
    tm = 1024
    return pl.pallas_call(
        _rope_table_kernel,
        out_shape=(jax.ShapeDtypeStruct((n, LANES), F32), jax.ShapeDtypeStruct((n, LANES), F32)),
        grid=(n // tm,),
        in_specs=[pl.BlockSpec((tm, 1), lambda i: (i, 0)),
                  pl.BlockSpec((1, LANES), lambda i: (0, 0))],
        out_specs=(pl.BlockSpec((tm, LANES), lambda i: (i, 0)),
                   pl.BlockSpec((tm, LANES), lambda i: (i, 0))),
        compiler_params=pltpu.CompilerParams(dimension_semantics=("arbitrary",)),
        name="rope_tables",
    )(pos_col, invf_row)


def _qkv_kernel(x_ref, w_ref, cos_ref, sin_ref, sbq, sbk, sbv, daq, dak, dav):
    xb = x_ref[...].astype(BF16)
    lane = lax.broadcasted_iota(jnp.int32, (1, LANES), 1) % HEAD_DIM
    half = ROPE_DIM // 2
    cosr = jnp.where(lane < ROPE_DIM, cos_ref[...], 1.0)
    sin_lo = jnp.where(lane < half, -sin_ref[...], 0.0)
    sin_hi = jnp.where((lane >= half) & (lane < ROPE_DIM), sin_ref[...], 0.0)
    scale = HEAD_DIM ** -0.5
    outs = (sbq, sbk, sbv, daq, dak, dav)
    for c, o_ref in enumerate(outs):
        hc = _dot(xb, w_ref[:, c * SB_WIDTH:(c + 1) * SB_WIDTH])
        for blk in range(SB_WIDTH // LANES):
            h = hc[:, blk * LANES:(blk + 1) * LANES]
            if c in (3, 4):
                h = (h * cosr + pltpu.roll(h, LANES - half, 1) * sin_lo
                     + pltpu.roll(h, half, 1) * sin_hi)
            if c in (0, 3):
                h = h * scale
            o_ref[:, blk * LANES:(blk + 1) * LANES] = h.astype(BF16)


def _qkv(x2d, w_bf16, cos_t, sin_t, layer):
    n = x2d.shape[0]
    tm = TM_QKV
    row = lambda i: (i, 0)
    out_sd = jax.ShapeDtypeStruct((n, SB_WIDTH), BF16)
    return pl.pallas_call(
        _qkv_kernel,
        out_shape=(out_sd,) * 6,
        grid=(n // tm,),
        in_specs=[pl.BlockSpec((tm, D_MODEL), row),
                  pl.BlockSpec((None,) + w_bf16.shape[1:], lambda i: (layer, 0, 0)),
                  pl.BlockSpec((tm, LANES), row),
                  pl.BlockSpec((tm, LANES), row)],
        out_specs=(pl.BlockSpec((tm, SB_WIDTH), row),) * 6,
        compiler_params=pltpu.CompilerParams(dimension_semantics=("arbitrary",),
                                             vmem_limit_bytes=VMEM_LIMIT),
        name="qkv",
    )(x2d, w_bf16, cos_t, sin_t)


def _stack_halves(q):
    first = lax.broadcasted_iota(jnp.int32, (1, LANES), 1) < HEAD_DIM
    zero = jnp.zeros_like(q)
    return jnp.concatenate([jnp.where(first, q, zero), jnp.where(first, zero, q)], axis=0)


def _local_positions(rows, bq, bk):
    r = lax.broadcasted_iota(jnp.int32, (rows, bk), 0)
    return jnp.where(r >= bq, r - bq, r), lax.broadcasted_iota(jnp.int32, (rows, bk), 1)


def _sb_kernel(q_ref, k_ref, v_ref, u_ref, g_ref, o_ref):
    qi = pl.program_id(2)
    qs = _stack_halves(q_ref[...])
    rows = 2 * SB_BQ
    t_local, s_local = _local_positions(rows, SB_BQ, SB_BK)
    u = u_ref[...]

    def block(qm, kb, carry, acc, mask):
        start = pl.multiple_of(kb * SB_BK, SB_BK)
        ks = k_ref[pl.ds(start, SB_BK), :]
        vs = v_ref[pl.ds(start, SB_BK), :]
        z = _dot_nt(qm, ks)
        sp = jnp.maximum(z, 0.0) + jnp.log(1.0 + jnp.exp2(jnp.abs(z) * (-LOG2E)))
        if mask is not None:
            sp = jnp.where(mask, sp, 0.0)
        tail = _dot(sp.astype(BF16), u)
        a = jnp.exp(z - tail - carry)
        if mask is not None:
            a = jnp.where(mask, a, 0.0)
        acc = acc + _dot(a.astype(BF16), vs)
        carry = carry + jnp.sum(sp, axis=1, keepdims=True)
        return carry, acc

    carry, acc = jnp.zeros((rows, 1), F32), jnp.zeros((rows, LANES), F32)
    per_q = SB_BQ // SB_BK
    for j in reversed(range(per_q)):
        carry, acc = block(qs, qi * per_q + j, carry, acc, s_local + j * SB_BK < t_local)

    def more(st):
        return (st[0] >= 0) & (jnp.min(st[1]) < SB_UNDERFLOW)

    def step(st):
        carry, acc = block(qs, st[0], st[1], st[2], None)
        return st[0] - 1, carry, acc

    _, carry, acc = lax.while_loop(more, step, (qi * per_q - 1, carry, acc))

    first = lax.broadcasted_iota(jnp.int32, (1, LANES), 1) < HEAD_DIM
    o = jnp.where(first, acc[:SB_BQ], acc[SB_BQ:])
    sq = o * o
    ss0 = jnp.sum(jnp.where(first, sq, 0.0), axis=1, keepdims=True)
    ss1 = jnp.sum(jnp.where(first, 0.0, sq), axis=1, keepdims=True)
    ms = jnp.where(first, ss0, ss1) * (1.0 / HEAD_DIM)
    o_ref[...] = (o * lax.rsqrt(ms + LN_EPS) * g_ref[...]).astype(BF16)


def _sb_attention(q, k, v, u_tri, g_row, batch, seq):
    nq = seq // SB_BQ
    n_pairs = SB_WIDTH // LANES
    return pl.pallas_call(
        _sb_kernel,
        out_shape=jax.ShapeDtypeStruct(q.shape, BF16),
        grid=(batch, n_pairs, nq),
        in_specs=[pl.BlockSpec((SB_BQ, LANES), lambda b, h, i: (b * nq + i, h)),
                  pl.BlockSpec((seq, LANES), lambda b, h, i: (b, h)),
                  pl.BlockSpec((seq, LANES), lambda b, h, i: (b, h)),
                  pl.BlockSpec((SB_BK, SB_BK), lambda b, h, i: (0, 0)),
                  pl.BlockSpec((1, LANES), lambda b, h, i: (0, 0))],
        out_specs=pl.BlockSpec((SB_BQ, LANES), lambda b, h, i: (b * nq + i, h)),
        compiler_params=pltpu.CompilerParams(
            dimension_semantics=("arbitrary", "arbitrary", "arbitrary"), vmem_limit_bytes=VMEM_LIMIT),
        name="sb_attn",
    )(q, k, v, u_tri, g_row)


def _da_kernel(q_ref, k_ref, v_ref, lam_ref, g_ref, o_ref, *, lambda_init):
    qi = pl.program_id(2)
    qs = _stack_halves(q_ref[...])
    rows = 2 * DA_BQ
    t_local, s_local = _local_positions(rows, DA_BQ, DA_BK)

    def block(kb, m, l, acc, mask, bk=DA_BK):
        start = pl.multiple_of(kb * bk, bk)
        ks = k_ref[pl.ds(start, bk), :]
        vs = v_ref[pl.ds(start, bk), :]
        s = _dot_nt(qs, ks)
        if mask is not None:
            s = jnp.where(mask, s, -jnp.inf)
        m_blk = jnp.max(s, axis=1, keepdims=True)
        m_new = m_blk if m is None else jnp.maximum(m, m_blk)
        p = jnp.exp(s - m_new)
        pv = _dot(p.astype(BF16), vs)
        ps = jnp.sum(p, axis=1, keepdims=True)
        if m is None:
            return m_new, ps, pv
        alpha = jnp.exp(m - m_new)
        return m_new, alpha * l + ps, alpha * acc + pv

    per_q = DA_BQ // DA_BK
    m = l = acc = None
    for j in range(per_q):
        m, l, acc = block(qi * per_q + j, m, l, acc, s_local + j * DA_BK <= t_local)
    wide = DA_BQ // DA_BK_WIDE
    m, l, acc = lax.fori_loop(0, qi * wide,
                              lambda kb, st: block(kb, st[0], st[1], st[2], None, DA_BK_WIDE), (m, l, acc))
    o_maps = acc / l

    lp = lam_ref[...]
    lam = (jnp.exp(jnp.sum(lp[0:1] * lp[1:2], keepdims=True))
           - jnp.exp(jnp.sum(lp[2:3] * lp[3:4], keepdims=True)) + lambda_init)
    o = o_maps[:DA_BQ] - lam * o_maps[DA_BQ:]
    ms = jnp.mean(o * o, axis=1, keepdims=True)
    o_ref[...] = (o * lax.rsqrt(ms + LN_EPS) * (g_ref[...] * (1.0 - lambda_init))).astype(BF16)


def _da_attention(q, k, v, lam_p, g_row, batch, seq, lambda_init):
    nq = seq // DA_BQ
    return pl.pallas_call(
        functools.partial(_da_kernel, lambda_init=lambda_init),
        out_shape=jax.ShapeDtypeStruct(q.shape, BF16),
        grid=(batch, DA_HEADS, nq),
        in_specs=[pl.BlockSpec((DA_BQ, LANES), lambda b, h, i: (b * nq + i, h)),
                  pl.BlockSpec((seq, LANES), lambda b, h, i: (b, h)),
                  pl.BlockSpec((seq, LANES), lambda b, h, i: (b, h)),
                  pl.BlockSpec(lam_p.shape, lambda b, h, i: (0, 0)),
                  pl.BlockSpec((1, LANES), lambda b, h, i: (0, 0))],
        out_specs=pl.BlockSpec((DA_BQ, LANES), lambda b, h, i: (b * nq + i, h)),
        compiler_params=pltpu.CompilerParams(
            dimension_semantics=("arbitrary", "arbitrary", "arbitrary"), vmem_limit_bytes=VMEM_LIMIT),
        name="da_attn",
    )(q, k, v, lam_p, g_row)


def _route_kernel(sb_ref, da_ref, x_ref, wot_ref, wob_ref, g_ref, b_ref, rw_ref, rb_ref, u_ref,
                  xg_ref, cls_ref, rank_ref, cnt_ref, *, alpha):
    tm = x_ref.shape[0]

    @pl.when(pl.program_id(0) == 0)
    def _():
        cnt_ref[...] = jnp.zeros_like(cnt_ref)

    mix = _dot(sb_ref[...], wot_ref[...]) + _dot(da_ref[...], wob_ref[...])
    x1 = _layer_norm(alpha * x_ref[...] + mix, g_ref[...], b_ref[...])
    xg_ref[:, :D_MODEL] = x1

    x_hi = x1.astype(BF16)
    x_lo = (x1 - x_hi.astype(F32)).astype(BF16)
    by_hi = _dot_nt(rw_ref[...], x_hi)
    logits = (by_hi[0:ROUTER_ROWS] + by_hi[ROUTER_ROWS:]) + _dot_nt(rw_ref[0:ROUTER_ROWS], x_lo)
    scores = _sigmoid(logits)
    sel = scores + rb_ref[...]
    grow = lax.broadcasted_iota(jnp.int32, (8, tm), 0)
    a, b, c, d = sel[0:8], sel[8:16], sel[16:24], sel[24:32]
    hi1, lo1 = jnp.maximum(a, b), jnp.minimum(a, b)
    hi2, lo2 = jnp.maximum(c, d), jnp.minimum(c, d)
    top1 = jnp.maximum(hi1, hi2)
    top2 = jnp.maximum(jnp.minimum(hi1, hi2), jnp.maximum(lo1, lo2))
    gscore = jnp.where(grow < N_GROUPS, top1 + top2, -jnp.inf)
    gmax = jnp.max(gscore, axis=0, keepdims=True)
    grp = jnp.min(jnp.where(gscore == gmax, grow, 8), axis=0, keepdims=True)
    gsel = grow == grp

    one = lambda m: jnp.where(m, 1, 0)
    ranks = (one(b > a) + one(c > a) + one(d > a),
             one(a >= b) + one(c > b) + one(d > b),
             one(a >= c) + one(b >= c) + one(d > c),
             one(a >= d) + one(b >= d) + one(c >= d))
    sc = (scores[0:8], scores[8:16], scores[16:24], scores[24:32])
    chosen = [jnp.where(gsel, one(r < TOP_K), 0) for r in ranks]
    picked = [jnp.sum(ch, axis=0, keepdims=True) for ch in chosen]
    num = [jnp.sum(jnp.where(ch == 1, s, 0.0), axis=0, keepdims=True) for ch, s in zip(chosen, sc)]
    den = num[0] + num[1] + num[2] + num[3]
    gate = [n_ / den for n_ in num]
    g_first = jnp.where(picked[0] == 1, gate[0], jnp.where(picked[1] == 1, gate[1], gate[2]))
    g_second = jnp.where(picked[3] == 1, gate[3], jnp.where(picked[2] == 1, gate[2], gate[1]))
    code = picked[0] + 2 * picked[1] + 4 * picked[2] + 8 * picked[3]
    pair = jnp.where(code == 3, 0, jnp.where(code == 5, 1, jnp.where(code == 6, 2,
           jnp.where(code == 9, 3, jnp.where(code == 10, 4, 5)))))
    cls = grp * N_PAIRS + pair

    crow = lax.broadcasted_iota(jnp.int32, (CLASS_ROWS, tm), 0)
    onehot = jnp.where(crow == cls, 1.0, 0.0)
    prefix = _dot(onehot.astype(BF16), u_ref[...])
    before = cnt_ref[:, 0:1]
    rank = jnp.sum(onehot * (prefix + before), axis=0, keepdims=True) - 1.0
    cnt_ref[...] = cnt_ref[...] + jnp.sum(onehot, axis=1, keepdims=True)
    cls_ref[0] = cls
    rank_ref[0] = rank.astype(jnp.int32)

    trow = lax.broadcasted_iota(jnp.int32, (GATE_COLS, tm), 0)
    gates_t = jnp.where(trow == 0, g_first, jnp.where(trow == 1, g_second, 0.0))
    xg_ref[:, D_MODEL:] = gates_t.T


TOP_K = 2


def _proj_route(sb, da, x2d, wo_top, wo_bot, g, b, rw_t, rb_col, u_tri, alpha):
    n = x2d.shape[0]
    tm = TM_ROUTE
    nt = n // tm
    row = lambda i: (i, 0)
    const = lambda i: (0, 0)
    return pl.pallas_call(
        functools.partial(_route_kernel, alpha=alpha),
        out_shape=(jax.ShapeDtypeStruct((n, ROW_W), F32),
                   jax.ShapeDtypeStruct((nt, 1, tm), jnp.int32),
                   jax.ShapeDtypeStruct((nt, 1, tm), jnp.int32),
                   jax.ShapeDtypeStruct((CLASS_ROWS, LANES), F32)),
        grid=(nt,),
        in_specs=[pl.BlockSpec((tm, SB_WIDTH), row), pl.BlockSpec((tm, DA_WIDTH), row),
                  pl.BlockSpec((tm, D_MODEL), row),
                  pl.BlockSpec(wo_top.shape, const), pl.BlockSpec(wo_bot.shape, const),
                  pl.BlockSpec((1, D_MODEL), const), pl.BlockSpec((1, D_MODEL), const),
                  pl.BlockSpec(rw_t.shape, const), pl.BlockSpec(rb_col.shape, const),
                  pl.BlockSpec((tm, tm), const)],
        out_specs=(pl.BlockSpec((tm, ROW_W), row),
                   pl.BlockSpec((1, 1, tm), lambda i: (i, 0, 0)),
                   pl.BlockSpec((1, 1, tm), lambda i: (i, 0, 0)),
                   pl.BlockSpec((CLASS_ROWS, LANES), const)),
        compiler_params=pltpu.CompilerParams(dimension_semantics=("arbitrary",),
                                             vmem_limit_bytes=VMEM_LIMIT),
        name="proj_route",
    )(sb, da, x2d, wo_top, wo_bot, g, b, rw_t, rb_col, u_tri)


def _dispatch_kernel(pos_ref, xg_ref, zeros_ref, xs_ref, sem):
    del zeros_ref
    groups = xg_ref.shape[0]
    tm = groups * ROW_UNROLL
    base = pl.program_id(0) * tm

    def issue(g, _):
        for k in range(ROW_UNROLL):
            pltpu.make_async_copy(xg_ref.at[g, pl.ds(k, 1), :],
                                  xs_ref.at[pl.ds(pos_ref[base + g * ROW_UNROLL + k], 1), :],
                                  sem).start(priority=k % 2)
        return 0

    lax.fori_loop(0, groups, issue, 0)
    whole = xs_ref.at[pl.ds(0, tm), :]
    pltpu.make_async_copy(whole, whole, sem).wait()


def _dispatch(pos, xg, n_rows):
    n = xg.shape[0]
    tm = TM_DISPATCH
    assert ROW_UNROLL == 8
    xg = xg.reshape(n // ROW_UNROLL, ROW_UNROLL, ROW_W)
    zeros = jnp.zeros((n_rows, ROW_W), F32)
    return pl.pallas_call(
        _dispatch_kernel,
        out_shape=jax.ShapeDtypeStruct((n_rows, ROW_W), F32),
        grid_spec=pltpu.PrefetchScalarGridSpec(
            num_scalar_prefetch=1,
            grid=(n // tm,),
            in_specs=[pl.BlockSpec((tm // ROW_UNROLL, ROW_UNROLL, ROW_W), lambda i, pos: (i, 0, 0)),
                      pl.BlockSpec(memory_space=pl.ANY)],
            out_specs=pl.BlockSpec(memory_space=pl.ANY),
            scratch_shapes=[pltpu.SemaphoreType.DMA]),
        input_output_aliases={2: 0},
        compiler_params=pltpu.CompilerParams(dimension_semantics=("arbitrary",),
                                             vmem_limit_bytes=VMEM_LIMIT),
        name="dispatch",
    )(pos, xg, zeros)


def _moe_kernel(e1_ref, e2_ref, nv_ref, xs_ref, wg1, wu1, wd1, wg2, wu2, wd2, ys_ref, *, alpha):
    @pl.when(pl.program_id(0) < nv_ref[0])
    def _():
        x = xs_ref[:, :D_MODEL]
        gates = xs_ref[:, D_MODEL:]
        xb = x.astype(BF16)

        def hidden(wg, wu):
            gt = _dot(xb, wg[...])
            return (gt * _sigmoid(gt) * _dot(xb, wu[...])).astype(BF16)

        h1, h2 = hidden(wg1, wu1), hidden(wg2, wu2)
        ffn = gates[:, 0:1] * _dot(h1, wd1[...]) + gates[:, 1:2] * _dot(h2, wd2[...])
        ys_ref[...] = alpha * x + ffn

    @pl.when(pl.program_id(0) >= nv_ref[0])
    def _():
        ys_ref[...] = jnp.zeros_like(ys_ref)


def _moe(tile_e1, tile_e2, n_valid, xs, wg, wu, wd, layer, alpha):
    n_rows = xs.shape[0]
    n_tiles = n_rows // MOE_TILE
    tile = lambda i, e1, e2, nv: (jnp.maximum(jnp.minimum(i, nv[0] - 1), 0), 0)
    first = lambda i, e1, e2, nv: (layer, e1[i], 0, 0)
    second = lambda i, e1, e2, nv: (layer, e2[i], 0, 0)
    up_spec = lambda f: pl.BlockSpec((None, None, D_MODEL, D_FF), f)
    down_spec = lambda f: pl.BlockSpec((None, None, D_FF, D_MODEL), f)
    return pl.pallas_call(
        functools.partial(_moe_kernel, alpha=alpha),
        out_shape=jax.ShapeDtypeStruct((n_rows, D_MODEL), F32),
        grid_spec=pltpu.PrefetchScalarGridSpec(
            num_scalar_prefetch=3,
            grid=(n_tiles,),
            in_specs=[pl.BlockSpec((MOE_TILE, ROW_W), tile),
                      up_spec(first), up_spec(first), down_spec(first),
                      up_spec(second), up_spec(second), down_spec(second)],
            out_specs=pl.BlockSpec((MOE_TILE, D_MODEL), lambda i, e1, e2, nv: (i, 0))),
        compiler_params=pltpu.CompilerParams(dimension_semantics=("arbitrary",),
                                             vmem_limit_bytes=VMEM_LIMIT),
        name="moe",
    )(tile_e1, tile_e2, n_valid, xs, wg, wu, wd, wg, wu, wd)


def _combine_kernel(pos_ref, ys_ref, p_ref, g_ref, b_ref, wpg_ref, bpg_ref, wple_ref, o_ref, buf, sem):
    tm = o_ref.shape[0]
    i = pl.program_id(0)
    slot = i % 2

    last = pl.num_programs(0) - 1

    def row_copy(tile, r, dst_slot):
        return pltpu.make_async_copy(ys_ref.at[pl.ds(pos_ref[tile * tm + r], 1), :],
                                     buf.at[dst_slot, pl.ds(r, 1), :], sem.at[dst_slot])

    def drain(s):
        pltpu.make_async_copy(ys_ref.at[pl.ds(0, tm), :], buf.at[s], sem.at[s]).wait()

    @pl.when(i == 0)
    def _():
        def issue(g, _):
            for k in range(ROW_UNROLL):
                row_copy(0, g * ROW_UNROLL + k, 0).start(priority=k % 2)
            return 0
        lax.fori_loop(0, tm // ROW_UNROLL, issue, 0)

    drain(slot)
    xin = buf[slot]
    nxt = jnp.minimum(i + 1, last)
    for r in range(tm):
        row_copy(nxt, r, 1 - slot).start(priority=r % 2)
    x2 = _layer_norm(xin, g_ref[...], b_ref[...])
    gate = _sigmoid(_dot(x2.astype(BF16), wpg_ref[...]) + bpg_ref[...])
    o_ref[...] = x2 + gate * _dot(p_ref[...].astype(BF16), wple_ref[...])

    @pl.when(i == last)
    def _():
        drain(1 - slot)


def _combine_ple(pos, ys, p3d, g, b, wpg, bpg, wple, layer):
    n = p3d.shape[1]
    tm = TM_ROWS
    row = lambda i, pos: (i, 0)
    const = lambda i, pos: (0, 0)
    layer_const = lambda i, pos: (layer, 0, 0)
    return pl.pallas_call(
        _combine_kernel,
        out_shape=jax.ShapeDtypeStruct((n, D_MODEL), F32),
        grid_spec=pltpu.PrefetchScalarGridSpec(
            num_scalar_prefetch=1,
            grid=(n // tm,),
            in_specs=[pl.BlockSpec(memory_space=pl.ANY),
                      pl.BlockSpec((None, tm, PLE_DIM), lambda i, pos: (layer, i, 0)),
                      pl.BlockSpec((1, D_MODEL), const), pl.BlockSpec((1, D_MODEL), const),
                      pl.BlockSpec((None,) + wpg.shape[1:], layer_const),
                      pl.BlockSpec((1, D_MODEL), const),
                      pl.BlockSpec((None,) + wple.shape[1:], layer_const)],
            out_specs=pl.BlockSpec((tm, D_MODEL), row),
            scratch_shapes=[pltpu.VMEM((2, tm, D_MODEL), F32), pltpu.SemaphoreType.DMA((2,))]),
        compiler_params=pltpu.CompilerParams(dimension_semantics=("arbitrary",),
                                             vmem_limit_bytes=VMEM_LIMIT),
        name="combine_ple",
    )(pos, ys, p3d, g, b, wpg, bpg, wple)


def _tri(n, inclusive_lower):
    j = np.arange(n)[:, None]
    s = np.arange(n)[None, :]
    return jnp.asarray((j >= s) if inclusive_lower else (j <= s), dtype=BF16)


def kernel(x, p, positions, w_in, w_o, sb_norm_g, da_lambda, da_subln_g, ln1_g, ln1_b, ln2_g, ln2_b,
           router_w, router_b, w_gate, w_up, w_down, w_ple, w_ple_gate, b_ple_gate):
    batch, seq, d = x.shape
    depth = w_in.shape[0]
    n = batch * seq
    alpha = (2 * depth) ** 0.25
    max_tiles = n // MOE_TILE + N_CLASSES
    n_rows = max_tiles * MOE_TILE

    inv_freq = ROPE_THETA ** (-jnp.arange(0, ROPE_DIM, 2, dtype=F32) / ROPE_DIM)
    invf_row = jnp.tile(jnp.concatenate([inv_freq, inv_freq, jnp.zeros((HEAD_DIM - ROPE_DIM,), F32)]),
                        LANES // HEAD_DIM)[None, :]
    cos_t, sin_t = _rope_tables(positions.reshape(n, 1), invf_row)

    src = np.array([[4 * g + l for g in range(N_GROUPS)] for l in range(EXPERTS_PER_GROUP)])
    rw_t = jnp.zeros((EXPERTS_PER_GROUP, 8, d), F32).at[:, :N_GROUPS, :].set(router_w.T[src])
    rw_t = rw_t.reshape(ROUTER_ROWS, d)
    rw_hi = rw_t.astype(BF16)
    rw_pad = jnp.concatenate([rw_hi, (rw_t - rw_hi.astype(F32)).astype(BF16)], axis=0)
    rb_col = jnp.zeros((EXPERTS_PER_GROUP, 8), F32).at[:, :N_GROUPS].set(router_b[src]).reshape(ROUTER_ROWS, 1)

    u_sb = _tri(SB_BK, True)
    u_rank = _tri(TM_ROUTE, False)
    class_e1 = jnp.asarray(_CLASS_E1)
    class_e2 = jnp.asarray(_CLASS_E2)

    w_in_b, wg_b, wu_b, wd_b = (w.astype(BF16) for w in (w_in, w_gate, w_up, w_down))
    wpg_b, wple_b = w_ple_gate.astype(BF16), w_ple.astype(BF16)
    p3d = p.reshape(depth, n, PLE_DIM)

    x2d = x.reshape(n, d)
    for i in range(depth):
        lambda_init = 0.8 - 0.6 * math.exp(-0.3 * i)
        sbq, sbk, sbv, daq, dak, dav = _qkv(x2d, w_in_b, cos_t, sin_t, i)
        g_sb = jnp.tile(sb_norm_g[i], LANES // HEAD_DIM)[None, :]
        sb = _sb_attention(sbq, sbk, sbv, u_sb, g_sb, batch, seq)
        da = _da_attention(daq, dak, dav, da_lambda[i], da_subln_g[i][None, :], batch, seq, lambda_init)

        wo = w_o[i].astype(BF16)
        xg, cls, rank, counts = _proj_route(sb, da, x2d, wo[:SB_WIDTH], wo[SB_WIDTH:],
                                            ln1_g[i][None, :], ln1_b[i][None, :], rw_pad, rb_col, u_rank, alpha)

        cnt = counts[:N_CLASSES, 0].astype(jnp.int32)
        tiles_c = (cnt + MOE_TILE - 1) // MOE_TILE
        tile_end = jnp.cumsum(tiles_c)
        row_off = (tile_end - tiles_c) * MOE_TILE
        pos = row_off[cls.reshape(n)] + rank.reshape(n)
        n_valid = tile_end[-1:]
        tile_ids = jnp.minimum(jnp.arange(max_tiles, dtype=jnp.int32), n_valid[0] - 1)
        tile_cls = jnp.minimum(jnp.sum((tile_ids[:, None] >= tile_end[None, :]).astype(jnp.int32), axis=1),
                               N_CLASSES - 1)

        xs = _dispatch(pos, xg, n_rows)
        ys = _moe(class_e1[tile_cls], class_e2[tile_cls], n_valid.astype(jnp.int32), xs,
                  wg_b, wu_b, wd_b, i, alpha)
        x2d = _combine_ple(pos, ys, p3d, ln2_g[i][None, :], ln2_b[i][None, :],
                           wpg_b, b_ple_gate[i][None, :], wple_b, i)
    return x2d.reshape(batch, seq, d)
```

```python
import functools
import math

import numpy as np
import jax
import jax.numpy as jnp
from jax import lax
from jax.experimental import pallas as pl
from jax.experimental.pallas import tpu as pltpu

D_MODEL = 1024
HEAD_DIM = 64
SB_WIDTH = 512
DA_WIDTH = 512
DA_HEADS = 4
ROPE_DIM = 16
ROPE_THETA = 500000.0
N_EXPERTS = 16
N_GROUPS = 4
EXPERTS_PER_GROUP = 4
D_FF = 512
PLE_DIM = 256
LN_EPS = 1e-5

LANES = 128
N_PAIRS = 6
N_CLASSES = N_GROUPS * N_PAIRS
CLASS_ROWS = 32
ROUTER_ROWS = 4 * 8
ROW_UNROLL = 8
GATE_COLS = LANES
ROW_W = D_MODEL + GATE_COLS

MOE_TILE = 256
SB_BQ = 512
SB_BK = 256
DA_BQ = 1024
DA_BK = 512
DA_BK_WIDE = 1024
SB_UNDERFLOW = 104.0
LOG2E = 1.4426950408889634
TM_QKV = 512
TM_ROUTE = 512
TM_DISPATCH = 4096
TM_ROWS = 512
VMEM_LIMIT = 56 * 1024 * 1024

F32 = jnp.float32
BF16 = jnp.bfloat16

_PAIR_LOCAL = [(0, 1), (0, 2), (1, 2), (0, 3), (1, 3), (2, 3)]
_CLASS_E1 = np.array([g * 4 + _PAIR_LOCAL[p][0] for g in range(N_GROUPS) for p in range(N_PAIRS)], np.int32)
_CLASS_E2 = np.array([g * 4 + _PAIR_LOCAL[p][1] for g in range(N_GROUPS) for p in range(N_PAIRS)], np.int32)


def _dot(a, b):
    return jnp.dot(a, b, preferred_element_type=F32)


def _dot_nt(a, b, precision=None):
    return lax.dot_general(a, b, (((1,), (1,)), ((), ())), precision=precision,
                           preferred_element_type=F32)


def _sigmoid(x):
    return 1.0 / (1.0 + jnp.exp(-x))


def _layer_norm(y, g, b):
    mu = jnp.mean(y, axis=-1, keepdims=True)
    yc = y - mu
    var = jnp.mean(yc * yc, axis=-1, keepdims=True)
    return yc * lax.rsqrt(var + LN_EPS) * g + b


def _rope_table_kernel(pos_ref, invf_ref, cos_ref, sin_ref):
    ang = pos_ref[...].astype(F32) * invf_ref[...]
    cos_ref[...] = jnp.cos(ang)
    sin_ref[...] = jnp.sin(ang)


def _rope_tables(pos_col, invf_row):
    n = pos_col.shape[0]
    tm = 1024
    return pl.pallas_call(
        _rope_table_kernel,
        out_shape=(jax.ShapeDtypeStruct((n, LANES), F32), jax.ShapeDtypeStruct((n, LANES), F32)),
        grid=(n // tm,),
        in_specs=[pl.BlockSpec((tm, 1), lambda i: (i, 0)),
                  pl.BlockSpec((1, LANES), lambda i: (0, 0))],
        out_specs=(pl.BlockSpec((tm, LANES), lambda i: (i, 0)),
                   pl.BlockSpec((tm, LANES), lambda i: (i, 0))),
        compiler_params=pltpu.CompilerParams(dimension_semantics=("arbitrary",)),
        name="rope_tables",
    )(pos_col, invf_row)


def _qkv_kernel(x_ref, w_ref, pos_ref, invf_ref, sbq, sbk, sbv, daq, dak, dav):
    xb = x_ref[...].astype(BF16)
    lane = lax.broadcasted_iota(jnp.int32, (1, LANES), 1) % HEAD_DIM
    half = ROPE_DIM // 2
    ang = pos_ref[...].astype(F32) * invf_ref[...]
    cos_a, sin_a = jnp.cos(ang), jnp.sin(ang)
    cosr = jnp.where(lane < ROPE_DIM, cos_a, 1.0)
    sin_lo = jnp.where(lane < half, -sin_a, 0.0)
    sin_hi = jnp.where((lane >= half) & (lane < ROPE_DIM), sin_a, 0.0)
    scale = HEAD_DIM ** -0.5
    outs = (sbq, sbk, sbv, daq, dak, dav)
    for c, o_ref in enumerate(outs):
        hc = _dot(xb, w_ref[:, c * SB_WIDTH:(c + 1) * SB_WIDTH])
        for blk in range(SB_WIDTH // LANES):
            h = hc[:, blk * LANES:(blk + 1) * LANES]
            if c in (3, 4):
                h = (h * cosr + pltpu.roll(h, LANES - half, 1) * sin_lo
                     + pltpu.roll(h, half, 1) * sin_hi)
            if c in (0, 3):
                h = h * scale
            o_ref[:, blk * LANES:(blk + 1) * LANES] = h.astype(BF16)


def _qkv(x2d, w_bf16, pos_col, invf_row, layer):
    n = x2d.shape[0]
    tm = TM_QKV
    row = lambda i: (i, 0)
    out_sd = jax.ShapeDtypeStruct((n, SB_WIDTH), BF16)
    return pl.pallas_call(
        _qkv_kernel,
        out_shape=(out_sd,) * 6,
        grid=(n // tm,),
        in_specs=[pl.BlockSpec((tm, D_MODEL), row),
                  pl.BlockSpec((None,) + w_bf16.shape[1:], lambda i: (layer, 0, 0)),
                  pl.BlockSpec((tm, 1), row),
                  pl.BlockSpec((1, LANES), lambda i: (0, 0))],
        out_specs=(pl.BlockSpec((tm, SB_WIDTH), row),) * 6,
        compiler_params=pltpu.CompilerParams(dimension_semantics=("arbitrary",),
                                             vmem_limit_bytes=VMEM_LIMIT),
        name="qkv",
    )(x2d, w_bf16, pos_col, invf_row)


def _stack_halves(q):
    first = lax.broadcasted_iota(jnp.int32, (1, LANES), 1) < HEAD_DIM
    zero = jnp.zeros_like(q)
    return jnp.concatenate([jnp.where(first, q, zero), jnp.where(first, zero, q)], axis=0)


def _local_positions(rows, bq, bk):
    r = lax.broadcasted_iota(jnp.int32, (rows, bk), 0)
    return jnp.where(r >= bq, r - bq, r), lax.broadcasted_iota(jnp.int32, (rows, bk), 1)


def _sb_kernel(q_ref, k_ref, v_ref, u_ref, g_ref, o_ref):
    qi = pl.program_id(2)
    qs = _stack_halves(q_ref[...])
    rows = 2 * SB_BQ
    t_local, s_local = _local_positions(rows, SB_BQ, SB_BK)
    u = u_ref[...]

    def block(qm, kb, carry, acc, mask):
        start = pl.multiple_of(kb * SB_BK, SB_BK)
        ks = k_ref[pl.ds(start, SB_BK), :]
        vs = v_ref[pl.ds(start, SB_BK), :]
        z = _dot_nt(qm, ks)
        sp = jnp.maximum(z, 0.0) + jnp.log(1.0 + jnp.exp2(jnp.abs(z) * (-LOG2E)))
        if mask is not None:
            sp = jnp.where(mask, sp, 0.0)
        tail = _dot(sp.astype(BF16), u)
        a = jnp.exp(z - tail - carry)
        if mask is not None:
            a = jnp.where(mask, a, 0.0)
        acc = acc + _dot(a.astype(BF16), vs)
        carry = carry + jnp.sum(sp, axis=1, keepdims=True)
        return carry, acc

    carry, acc = jnp.zeros((rows, 1), F32), jnp.zeros((rows, LANES), F32)
    per_q = SB_BQ // SB_BK
    for j in reversed(range(per_q)):
        carry, acc = block(qs, qi * per_q + j, carry, acc, s_local + j * SB_BK < t_local)

    def more(st):
        return (st[0] >= 0) & (jnp.min(st[1]) < SB_UNDERFLOW)

    def step(st):
        carry, acc = block(qs, st[0], st[1], st[2], None)
        return st[0] - 1, carry, acc

    _, carry, acc = lax.while_loop(more, step, (qi * per_q - 1, carry, acc))

    first = lax.broadcasted_iota(jnp.int32, (1, LANES), 1) < HEAD_DIM
    o = jnp.where(first, acc[:SB_BQ], acc[SB_BQ:])
    sq = o * o
    ss0 = jnp.sum(jnp.where(first, sq, 0.0), axis=1, keepdims=True)
    ss1 = jnp.sum(jnp.where(first, 0.0, sq), axis=1, keepdims=True)
    ms = jnp.where(first, ss0, ss1) * (1.0 / HEAD_DIM)
    o_ref[...] = (o * lax.rsqrt(ms + LN_EPS) * g_ref[...]).astype(BF16)


def _sb_attention(q, k, v, u_tri, g_row, batch, seq):
    nq = seq // SB_BQ
    n_pairs = SB_WIDTH // LANES
    return pl.pallas_call(
        _sb_kernel,
        out_shape=jax.ShapeDtypeStruct(q.shape, BF16),
        grid=(batch, n_pairs, nq),
        in_specs=[pl.BlockSpec((SB_BQ, LANES), lambda b, h, i: (b * nq + i, h)),
                  pl.BlockSpec((seq, LANES), lambda b, h, i: (b, h)),
                  pl.BlockSpec((seq, LANES), lambda b, h, i: (b, h)),
                  pl.BlockSpec((SB_BK, SB_BK), lambda b, h, i: (0, 0)),
                  pl.BlockSpec((1, LANES), lambda b, h, i: (0, 0))],
        out_specs=pl.BlockSpec((SB_BQ, LANES), lambda b, h, i: (b * nq + i, h)),
        compiler_params=pltpu.CompilerParams(
            dimension_semantics=("arbitrary", "arbitrary", "arbitrary"), vmem_limit_bytes=VMEM_LIMIT),
        name="sb_attn",
    )(q, k, v, u_tri, g_row)


def _da_kernel(q_ref, k_ref, v_ref, lam_ref, g_ref, o_ref, *, lambda_init):
    qi = pl.program_id(2)
    qs = _stack_halves(q_ref[...])
    rows = 2 * DA_BQ
    t_local, s_local = _local_positions(rows, DA_BQ, DA_BK)

    def block(kb, m, l, acc, mask, bk=DA_BK):
        start = pl.multiple_of(kb * bk, bk)
        ks = k_ref[pl.ds(start, bk), :]
        vs = v_ref[pl.ds(start, bk), :]
        s = _dot_nt(qs, ks)
        if mask is not None:
            s = jnp.where(mask, s, -jnp.inf)
        m_blk = jnp.max(s, axis=1, keepdims=True)
        m_new = m_blk if m is None else jnp.maximum(m, m_blk)
        p = jnp.exp(s - m_new)
        pv = _dot(p.astype(BF16), vs)
        ps = jnp.sum(p, axis=1, keepdims=True)
        if m is None:
            return m_new, ps, pv
        alpha = jnp.exp(m - m_new)
        return m_new, alpha * l + ps, alpha * acc + pv

    per_q = DA_BQ // DA_BK
    m = l = acc = None
    for j in range(per_q):
        m, l, acc = block(qi * per_q + j, m, l, acc, s_local + j * DA_BK <= t_local)
    wide = DA_BQ // DA_BK_WIDE
    m, l, acc = lax.fori_loop(0, qi * wide,
                              lambda kb, st: block(kb, st[0], st[1], st[2], None, DA_BK_WIDE), (m, l, acc))
    o_maps = acc / l

    lp = lam_ref[...]
    lam = (jnp.exp(jnp.sum(lp[0:1] * lp[1:2], keepdims=True))
           - jnp.exp(jnp.sum(lp[2:3] * lp[3:4], keepdims=True)) + lambda_init)
    o = o_maps[:DA_BQ] - lam * o_maps[DA_BQ:]
    ms = jnp.mean(o * o, axis=1, keepdims=True)
    o_ref[...] = (o * lax.rsqrt(ms + LN_EPS) * (g_ref[...] * (1.0 - lambda_init))).astype(BF16)


def _da_attention(q, k, v, lam_p, g_row, batch, seq, lambda_init):
    nq = seq // DA_BQ
    return pl.pallas_call(
        functools.partial(_da_kernel, lambda_init=lambda_init),
        out_shape=jax.ShapeDtypeStruct(q.shape, BF16),
        grid=(batch, DA_HEADS, nq),
        in_specs=[pl.BlockSpec((DA_BQ, LANES), lambda b, h, i: (b * nq + i, h)),
                  pl.BlockSpec((seq, LANES), lambda b, h, i: (b, h)),
                  pl.BlockSpec((seq, LANES), lambda b, h, i: (b, h)),
                  pl.BlockSpec(lam_p.shape, lambda b, h, i: (0, 0)),
                  pl.BlockSpec((1, LANES), lambda b, h, i: (0, 0))],
        out_specs=pl.BlockSpec((DA_BQ, LANES), lambda b, h, i: (b * nq + i, h)),
        compiler_params=pltpu.CompilerParams(
            dimension_semantics=("arbitrary", "arbitrary", "arbitrary"), vmem_limit_bytes=VMEM_LIMIT),
        name="da_attn",
    )(q, k, v, lam_p, g_row)


def _route_kernel(sb_ref, da_ref, x_ref, wot_ref, wob_ref, g_ref, b_ref, rw_ref, rb_ref, u_ref,
                  xg_ref, cls_ref, rank_ref, cnt_ref, *, alpha):
    tm = x_ref.shape[0]

    @pl.when(pl.program_id(0) == 0)
    def _():
        cnt_ref[...] = jnp.zeros_like(cnt_ref)

    mix = _dot(sb_ref[...], wot_ref[...]) + _dot(da_ref[...], wob_ref[...])
    x1 = _layer_norm(alpha * x_ref[...] + mix, g_ref[...], b_ref[...])
    xg_ref[:, :D_MODEL] = x1

    x_hi = x1.astype(BF16)
    x_lo = (x1 - x_hi.astype(F32)).astype(BF16)
    by_hi = _dot_nt(rw_ref[...], x_hi)
    logits = (by_hi[0:ROUTER_ROWS] + by_hi[ROUTER_ROWS:]) + _dot_nt(rw_ref[0:ROUTER_ROWS], x_lo)
    scores = _sigmoid(logits)
    sel = scores + rb_ref[...]
    grow = lax.broadcasted_iota(jnp.int32, (8, tm), 0)
    a, b, c, d = sel[0:8], sel[8:16], sel[16:24], sel[24:32]
    hi1, lo1 = jnp.maximum(a, b), jnp.minimum(a, b)
    hi2, lo2 = jnp.maximum(c, d), jnp.minimum(c, d)
    top1 = jnp.maximum(hi1, hi2)
    top2 = jnp.maximum(jnp.minimum(hi1, hi2), jnp.maximum(lo1, lo2))
    gscore = jnp.where(grow < N_GROUPS, top1 + top2, -jnp.inf)
    gmax = jnp.max(gscore, axis=0, keepdims=True)
    grp = jnp.min(jnp.where(gscore == gmax, grow, 8), axis=0, keepdims=True)
    gsel = grow == grp

    one = lambda m: jnp.where(m, 1, 0)
    ranks = (one(b > a) + one(c > a) + one(d > a),
             one(a >= b) + one(c > b) + one(d > b),
             one(a >= c) + one(b >= c) + one(d > c),
             one(a >= d) + one(b >= d) + one(c >= d))
    sc = (scores[0:8], scores[8:16], scores[16:24], scores[24:32])
    chosen = [jnp.where(gsel, one(r < TOP_K), 0) for r in ranks]
    picked = [jnp.sum(ch, axis=0, keepdims=True) for ch in chosen]
    num = [jnp.sum(jnp.where(ch == 1, s, 0.0), axis=0, keepdims=True) for ch, s in zip(chosen, sc)]
    den = num[0] + num[1] + num[2] + num[3]
    gate = [n_ / den for n_ in num]
    g_first = jnp.where(picked[0] == 1, gate[0], jnp.where(picked[1] == 1, gate[1], gate[2]))
    g_second = jnp.where(picked[3] == 1, gate[3], jnp.where(picked[2] == 1, gate[2], gate[1]))
    code = picked[0] + 2 * picked[1] + 4 * picked[2] + 8 * picked[3]
    pair = jnp.where(code == 3, 0, jnp.where(code == 5, 1, jnp.where(code == 6, 2,
           jnp.where(code == 9, 3, jnp.where(code == 10, 4, 5)))))
    cls = grp * N_PAIRS + pair

    crow = lax.broadcasted_iota(jnp.int32, (CLASS_ROWS, tm), 0)
    onehot = jnp.where(crow == cls, 1.0, 0.0)
    prefix = _dot(onehot.astype(BF16), u_ref[...])
    before = cnt_ref[:, 0:1]
    rank = jnp.sum(onehot * (prefix + before), axis=0, keepdims=True) - 1.0
    cnt_ref[...] = cnt_ref[...] + jnp.sum(onehot, axis=1, keepdims=True)
    cls_ref[0] = cls
    rank_ref[0] = rank.astype(jnp.int32)

    trow = lax.broadcasted_iota(jnp.int32, (GATE_COLS, tm), 0)
    gates_t = jnp.where(trow == 0, g_first, jnp.where(trow == 1, g_second, 0.0))
    xg_ref[:, D_MODEL:] = gates_t.T


TOP_K = 2


def _proj_route(sb, da, x2d, wo_top, wo_bot, g, b, rw_t, rb_col, u_tri, alpha):
    n = x2d.shape[0]
    tm = TM_ROUTE
    nt = n // tm
    row = lambda i: (i, 0)
    const = lambda i: (0, 0)
    return pl.pallas_call(
        functools.partial(_route_kernel, alpha=alpha),
        out_shape=(jax.ShapeDtypeStruct((n, ROW_W), F32),
                   jax.ShapeDtypeStruct((nt, 1, tm), jnp.int32),
                   jax.ShapeDtypeStruct((nt, 1, tm), jnp.int32),
                   jax.ShapeDtypeStruct((CLASS_ROWS, LANES), F32)),
        grid=(nt,),
        in_specs=[pl.BlockSpec((tm, SB_WIDTH), row), pl.BlockSpec((tm, DA_WIDTH), row),
                  pl.BlockSpec((tm, D_MODEL), row),
                  pl.BlockSpec(wo_top.shape, const), pl.BlockSpec(wo_bot.shape, const),
                  pl.BlockSpec((1, D_MODEL), const), pl.BlockSpec((1, D_MODEL), const),
                  pl.BlockSpec(rw_t.shape, const), pl.BlockSpec(rb_col.shape, const),
                  pl.BlockSpec((tm, tm), const)],
        out_specs=(pl.BlockSpec((tm, ROW_W), row),
                   pl.BlockSpec((1, 1, tm), lambda i: (i, 0, 0)),
                   pl.BlockSpec((1, 1, tm), lambda i: (i, 0, 0)),
                   pl.BlockSpec((CLASS_ROWS, LANES), const)),
        compiler_params=pltpu.CompilerParams(dimension_semantics=("arbitrary",),
                                             vmem_limit_bytes=VMEM_LIMIT),
        name="proj_route",
    )(sb, da, x2d, wo_top, wo_bot, g, b, rw_t, rb_col, u_tri)


def _dispatch_kernel(pos_ref, xg_ref, zeros_ref, xs_ref, sem):
    del zeros_ref
    groups = xg_ref.shape[0]
    tm = groups * ROW_UNROLL
    base = pl.program_id(0) * tm

    def issue(g, _):
        for k in range(ROW_UNROLL):
            pltpu.make_async_copy(xg_ref.at[g, pl.ds(k, 1), :],
                                  xs_ref.at[pl.ds(pos_ref[base + g * ROW_UNROLL + k], 1), :],
                                  sem).start(priority=k % 2)
        return 0

    lax.fori_loop(0, groups, issue, 0)
    whole = xs_ref.at[pl.ds(0, tm), :]
    pltpu.make_async_copy(whole, whole, sem).wait()


def _dispatch(pos, xg, n_rows):
    n = xg.shape[0]
    tm = TM_DISPATCH
    assert ROW_UNROLL == 8
    xg = xg.reshape(n // ROW_UNROLL, ROW_UNROLL, ROW_W)
    zeros = jnp.zeros((n_rows, ROW_W), F32)
    return pl.pallas_call(
        _dispatch_kernel,
        out_shape=jax.ShapeDtypeStruct((n_rows, ROW_W), F32),
        grid_spec=pltpu.PrefetchScalarGridSpec(
            num_scalar_prefetch=1,
            grid=(n // tm,),
            in_specs=[pl.BlockSpec((tm // ROW_UNROLL, ROW_UNROLL, ROW_W), lambda i, pos: (i, 0, 0)),
                      pl.BlockSpec(memory_space=pl.ANY)],
            out_specs=pl.BlockSpec(memory_space=pl.ANY),
            scratch_shapes=[pltpu.SemaphoreType.DMA]),
        input_output_aliases={2: 0},
        compiler_params=pltpu.CompilerParams(dimension_semantics=("arbitrary",),
                                             vmem_limit_bytes=VMEM_LIMIT),
        name="dispatch",
    )(pos, xg, zeros)


def _moe_kernel(e1_ref, e2_ref, nv_ref, xs_ref, wg1, wu1, wd1, wg2, wu2, wd2, ys_ref, *, alpha):
    @pl.when(pl.program_id(0) < nv_ref[0])
    def _():
        x = xs_ref[:, :D_MODEL]
        gates = xs_ref[:, D_MODEL:]
        xb = x.astype(BF16)

        def hidden(wg, wu):
            gt = _dot(xb, wg[...])
            return (gt * _sigmoid(gt) * _dot(xb, wu[...])).astype(BF16)

        h1, h2 = hidden(wg1, wu1), hidden(wg2, wu2)
        ffn = gates[:, 0:1] * _dot(h1, wd1[...]) + gates[:, 1:2] * _dot(h2, wd2[...])
        ys_ref[...] = alpha * x + ffn

    @pl.when(pl.program_id(0) >= nv_ref[0])
    def _():
        ys_ref[...] = jnp.zeros_like(ys_ref)


def _moe(tile_e1, tile_e2, n_valid, xs, wg, wu, wd, layer, alpha):
    n_rows = xs.shape[0]
    n_tiles = n_rows // MOE_TILE
    tile = lambda i, e1, e2, nv: (jnp.maximum(jnp.minimum(i, nv[0] - 1), 0), 0)
    first = lambda i, e1, e2, nv: (layer, e1[i], 0, 0)
    second = lambda i, e1, e2, nv: (layer, e2[i], 0, 0)
    up_spec = lambda f: pl.BlockSpec((None, None, D_MODEL, D_FF), f)
    down_spec = lambda f: pl.BlockSpec((None, None, D_FF, D_MODEL), f)
    return pl.pallas_call(
        functools.partial(_moe_kernel, alpha=alpha),
        out_shape=jax.ShapeDtypeStruct((n_rows, D_MODEL), F32),
        grid_spec=pltpu.PrefetchScalarGridSpec(
            num_scalar_prefetch=3,
            grid=(n_tiles,),
            in_specs=[pl.BlockSpec((MOE_TILE, ROW_W), tile),
                      up_spec(first), up_spec(first), down_spec(first),
                      up_spec(second), up_spec(second), down_spec(second)],
            out_specs=pl.BlockSpec((MOE_TILE, D_MODEL), lambda i, e1, e2, nv: (i, 0))),
        compiler_params=pltpu.CompilerParams(dimension_semantics=("arbitrary",),
                                             vmem_limit_bytes=VMEM_LIMIT),
        name="moe",
    )(tile_e1, tile_e2, n_valid, xs, wg, wu, wd, wg, wu, wd)


def _combine_kernel(pos_ref, ys_ref, p_ref, g_ref, b_ref, wpg_ref, bpg_ref, wple_ref, o_ref, buf, sem):
    tm = o_ref.shape[0]
    i = pl.program_id(0)
    slot = i % 2

    last = pl.num_programs(0) - 1

    def row_copy(tile, r, dst_slot):
        return pltpu.make_async_copy(ys_ref.at[pl.ds(pos_ref[tile * tm + r], 1), :],
                                     buf.at[dst_slot, pl.ds(r, 1), :], sem.at[dst_slot])

    def drain(s):
        pltpu.make_async_copy(ys_ref.at[pl.ds(0, tm), :], buf.at[s], sem.at[s]).wait()

    @pl.when(i == 0)
    def _():
        def issue(g, _):
            for k in range(ROW_UNROLL):
                row_copy(0, g * ROW_UNROLL + k, 0).start(priority=k % 2)
            return 0
        lax.fori_loop(0, tm // ROW_UNROLL, issue, 0)

    drain(slot)
    xin = buf[slot]
    nxt = jnp.minimum(i + 1, last)
    for r in range(tm):
        row_copy(nxt, r, 1 - slot).start(priority=r % 2)
    x2 = _layer_norm(xin, g_ref[...], b_ref[...])
    gate = _sigmoid(_dot(x2.astype(BF16), wpg_ref[...]) + bpg_ref[...])
    o_ref[...] = x2 + gate * _dot(p_ref[...].astype(BF16), wple_ref[...])

    @pl.when(i == last)
    def _():
        drain(1 - slot)


def _combine_ple(pos, ys, p3d, g, b, wpg, bpg, wple, layer):
    n = p3d.shape[1]
    tm = TM_ROWS
    row = lambda i, pos: (i, 0)
    const = lambda i, pos: (0, 0)
    layer_const = lambda i, pos: (layer, 0, 0)
    return pl.pallas_call(
        _combine_kernel,
        out_shape=jax.ShapeDtypeStruct((n, D_MODEL), F32),
        grid_spec=pltpu.PrefetchScalarGridSpec(
            num_scalar_prefetch=1,
            grid=(n // tm,),
            in_specs=[pl.BlockSpec(memory_space=pl.ANY),
                      pl.BlockSpec((None, tm, PLE_DIM), lambda i, pos: (layer, i, 0)),
                      pl.BlockSpec((1, D_MODEL), const), pl.BlockSpec((1, D_MODEL), const),
                      pl.BlockSpec((None,) + wpg.shape[1:], layer_const),
                      pl.BlockSpec((1, D_MODEL), const),
                      pl.BlockSpec((None,) + wple.shape[1:], layer_const)],
            out_specs=pl.BlockSpec((tm, D_MODEL), row),
            scratch_shapes=[pltpu.VMEM((2, tm, D_MODEL), F32), pltpu.SemaphoreType.DMA((2,))]),
        compiler_params=pltpu.CompilerParams(dimension_semantics=("arbitrary",),
                                             vmem_limit_bytes=VMEM_LIMIT),
        name="combine_ple",
    )(pos, ys, p3d, g, b, wpg, bpg, wple)


def _tri(n, inclusive_lower):
    j = np.arange(n)[:, None]
    s = np.arange(n)[None, :]
    return jnp.asarray((j >= s) if inclusive_lower else (j <= s), dtype=BF16)


def kernel(x, p, positions, w_in, w_o, sb_norm_g, da_lambda, da_subln_g, ln1_g, ln1_b, ln2_g, ln2_b,
           router_w, router_b, w_gate, w_up, w_down, w_ple, w_ple_gate, b_ple_gate):
    batch, seq, d = x.shape
    depth = w_in.shape[0]
    n = batch * seq
    alpha = (2 * depth) ** 0.25
    max_tiles = n // MOE_TILE + N_CLASSES
    n_rows = max_tiles * MOE_TILE

    inv_freq = ROPE_THETA ** (-jnp.arange(0, ROPE_DIM, 2, dtype=F32) / ROPE_DIM)
    invf_row = jnp.tile(jnp.concatenate([inv_freq, inv_freq, jnp.zeros((HEAD_DIM - ROPE_DIM,), F32)]),
                        LANES // HEAD_DIM)[None, :]
    pos_col = positions.reshape(n, 1)

    src = np.array([[4 * g + l for g in range(N_GROUPS)] for l in range(EXPERTS_PER_GROUP)])
    rw_t = jnp.zeros((EXPERTS_PER_GROUP, 8, d), F32).at[:, :N_GROUPS, :].set(router_w.T[src])
    rw_t = rw_t.reshape(ROUTER_ROWS, d)
    rw_hi = rw_t.astype(BF16)
    rw_pad = jnp.concatenate([rw_hi, (rw_t - rw_hi.astype(F32)).astype(BF16)], axis=0)
    rb_col = jnp.zeros((EXPERTS_PER_GROUP, 8), F32).at[:, :N_GROUPS].set(router_b[src]).reshape(ROUTER_ROWS, 1)

    u_sb = _tri(SB_BK, True)
    u_rank = _tri(TM_ROUTE, False)
    class_e1 = jnp.asarray(_CLASS_E1)
    class_e2 = jnp.asarray(_CLASS_E2)

    w_in_b, wg_b, wu_b, wd_b = (w.astype(BF16) for w in (w_in, w_gate, w_up, w_down))
    wpg_b, wple_b = w_ple_gate.astype(BF16), w_ple.astype(BF16)
    p3d = p.reshape(depth, n, PLE_DIM)

    x2d = x.reshape(n, d)
    for i in range(depth):
        lambda_init = 0.8 - 0.6 * math.exp(-0.3 * i)
        sbq, sbk, sbv, daq, dak, dav = _qkv(x2d, w_in_b, pos_col, invf_row, i)
        g_sb = jnp.tile(sb_norm_g[i], LANES // HEAD_DIM)[None, :]
        sb = _sb_attention(sbq, sbk, sbv, u_sb, g_sb, batch, seq)
        da = _da_attention(daq, dak, dav, da_lambda[i], da_subln_g[i][None, :], batch, seq, lambda_init)

        wo = w_o[i].astype(BF16)
        xg, cls, rank, counts = _proj_route(sb, da, x2d, wo[:SB_WIDTH], wo[SB_WIDTH:],
                                            ln1_g[i][None, :], ln1_b[i][None, :], rw_pad, rb_col, u_rank, alpha)

        cnt = counts[:N_CLASSES, 0].astype(jnp.int32)
        tiles_c = (cnt + MOE_TILE - 1) // MOE_TILE
        tile_end = jnp.cumsum(tiles_c)
        row_off = (tile_end - tiles_c) * MOE_TILE
        pos = row_off[cls.reshape(n)] + rank.reshape(n)
        n_valid = tile_end[-1:]
        tile_ids = jnp.minimum(jnp.arange(max_tiles, dtype=jnp.int32), n_valid[0] - 1)
        tile_cls = jnp.minimum(jnp.sum((tile_ids[:, None] >= tile_end[None, :]).astype(jnp.int32), axis=1),
                               N_CLASSES - 1)

        xs = _dispatch(pos, xg, n_rows)
        ys = _moe(class_e1[tile_cls], class_e2[tile_cls], n_valid.astype(jnp.int32), xs,
                  wg_b, wu_b, wd_b, i, alpha)
        x2d = _combine_ple(pos, ys, p3d, ln2_g[i][None, :], ln2_b[i][None, :],
                           wpg_b, b_ple_gate[i][None, :], wple_b, i)
    return x2d.reshape(batch, seq, d)
```
